```python
import math
import jax, jax.numpy as jnp
from jax import lax
import numpy as np

D_MODEL = 1024
BATCH = 32
SEQ = 2048
DEPTH = 2

HEAD_DIM = 64
A_WIDTH = 3 * D_MODEL // 8
HY_WIDTH = D_MODEL // 4
C_WIDTH = D_MODEL - A_WIDTH - HY_WIDTH
MIX_WIDTH = A_WIDTH + HY_WIDTH + C_WIDTH
A_HEADS = A_WIDTH // HEAD_DIM
DILATED_PATTERNS = ((128, 1), (512, 4), (2048, 16))
C_Q_HEADS = C_WIDTH // HEAD_DIM
C_GROUP = 3
C_KV_HEADS = C_Q_HEADS // C_GROUP
C_KV_WIDTH = C_KV_HEADS * HEAD_DIM
Q_BLOCK = 128
GRID_W = 64
ROPE_THETA = 10000.0
HY_ORDER = 2
HY_BANDS = 16
HY_EMB = 1 + 2 * HY_BANDS
HY_HIDDEN = 64
D_FF = 11 * D_MODEL // 4
EPS = 1e-6

A_Q0 = 0
A_K0 = A_Q0 + A_WIDTH
A_V0 = A_K0 + A_WIDTH
HY_0 = A_V0 + A_WIDTH
C_Q0 = HY_0 + (HY_ORDER + 1) * HY_WIDTH
C_K0 = C_Q0 + C_WIDTH
C_V0 = C_K0 + C_KV_WIDTH
PROJ_WIDTH = C_V0 + C_KV_WIDTH

kernel_name = "hybrid_dilated_hyena_axial_gqa_encoder"

F32 = jnp.float32


def rms_norm(x, g):
    xf = x.astype(F32)
    y = xf * lax.rsqrt(jnp.mean(xf * xf, axis=-1, keepdims=True) + EPS)
    return (y * g.astype(F32)).astype(x.dtype)


def rope_angles(pos, dim):
    freqs = ROPE_THETA ** (-jnp.arange(0, dim, 2, dtype=F32) / dim)
    ang = pos.astype(F32)[:, None] * freqs[None, :]
    return jnp.cos(ang), jnp.sin(ang)


def apply_rope(x, cos, sin):
    xf = x.astype(F32)
    half = x.shape[-1] // 2
    x1, x2 = xf[..., :half], xf[..., half:]
    c, s = cos[None, :, None, :], sin[None, :, None, :]
    return jnp.concatenate([x1 * c - x2 * s, x2 * c + x1 * s], axis=-1).astype(x.dtype)


def axial_rope(x, rows, cols):
    half = x.shape[-1] // 2
    cr, sr = rope_angles(rows, half)
    cc, sc = rope_angles(cols, half)
    return jnp.concatenate([apply_rope(x[..., :half], cr, sr), apply_rope(x[..., half:], cc, sc)], axis=-1)


def dwconv3(x, w, b):
    xp = jnp.pad(x, ((0, 0), (1, 1), (0, 0)))
    return xp[:, :-2] * w[0] + xp[:, 1:-1] * w[1] + xp[:, 2:] * w[2] + b


def dilated_branch(q, k, v, dilation, n_side):
    B, S, H, E = q.shape
    Ls = S // dilation
    blk = n_side
    nb = -(-Ls // blk)
    Lp = nb * blk

    def regroup(a):
        return a.reshape(B, Ls, dilation, H, E).transpose(0, 2, 3, 1, 4)

    qs = jnp.pad(regroup(q), ((0, 0), (0, 0), (0, 0), (0, Lp - Ls), (0, 0))).reshape(B, dilation, H, nb, blk, E)

    def windows(a):
        ap = jnp.pad(regroup(a), ((0, 0), (0, 0), (0, 0), (blk, Lp - Ls + blk), (0, 0)))
        ap = ap.reshape(B, dilation, H, nb + 2, blk, E)
        return jnp.concatenate([ap[:, :, :, :-2], ap[:, :, :, 1:-1], ap[:, :, :, 2:]], axis=4)

    kw, vw = windows(k), windows(v)
    qi = jnp.arange(nb)[:, None, None] * blk + jnp.arange(blk)[None, :, None]
    ki = jnp.arange(nb)[:, None, None] * blk - blk + jnp.arange(3 * blk)[None, None, :]
    valid = (jnp.abs(qi - ki) <= n_side) & (ki >= 0) & (ki < Ls)

    s = jnp.einsum('bdhnqe,bdhnke->bdhnqk', qs, kw).astype(F32) * (E ** -0.5)
    s = jnp.where(valid, s, -1e30)
    m = jnp.max(s, axis=-1, keepdims=True)
    p = jnp.exp(s - m)
    l = jnp.sum(p, axis=-1, keepdims=True)
    o = jnp.einsum('bdhnqk,bdhnke->bdhnqe', (p / l).astype(v.dtype), vw)
    lse = (m + jnp.log(l))[..., 0]

    o = o.reshape(B, dilation, H, Lp, E)[:, :, :, :Ls].transpose(0, 3, 1, 2, 4).reshape(B, S, H, E)
    lse = lse.reshape(B, dilation, H, Lp)[:, :, :, :Ls].transpose(0, 3, 1, 2).reshape(B, S, H)
    return o, lse


def dilated_attention(q, k, v):
    outs, lses = [], []
    for window, dilation in DILATED_PATTERNS:
        o, lse = dilated_branch(q, k, v, dilation, (window // 2) // dilation)
        outs.append(o)
        lses.append(lse)
    w = jax.nn.softmax(jnp.stack(lses, axis=0), axis=0)
    o = jnp.sum(w[..., None] * jnp.stack(outs, axis=0).astype(F32), axis=0)
    return o.astype(q.dtype)


def hyena_filters(L, w1, b1, freq, w2, b2, w3, decay):
    t = jnp.linspace(0.0, 1.0, L, dtype=F32)[:, None]
    bands = jnp.linspace(1e-4, HY_BANDS - 1, HY_BANDS, dtype=F32)
    ang = 2.0 * math.pi * bands[None, :] * jnp.arange(L, dtype=F32)[:, None] / L
    z = jnp.concatenate([t, jnp.cos(ang), -jnp.sin(ang)], axis=-1)
    freq = freq.astype(F32)
    h = jnp.sin(freq[0] * (z @ w1.astype(F32) + b1.astype(F32)))
    h = jnp.sin(freq[1] * (h @ w2.astype(F32) + b2.astype(F32)))
    h = (h @ w3.astype(F32)).reshape(L, HY_ORDER, 2, HY_WIDTH)
    h = h * jnp.exp(-t[:, :, None, None] * decay.astype(F32)[None])
    hf, hb = h[:, :, 0], h[:, :, 1]
    kc = jnp.concatenate([hf, jnp.zeros((1, HY_ORDER, HY_WIDTH), F32), hb[1:][::-1]], axis=0)
    kc = kc / jnp.sum(jnp.abs(kc), axis=0, keepdims=True)
    return jnp.fft.rfft(kc, axis=0)


def fftconv(u, kf, dbias):
    L = u.shape[1]
    uf = u.astype(F32)
    U = jnp.fft.rfft(uf, n=2 * L, axis=1)
    y = jnp.fft.irfft(U * kf[None], n=2 * L, axis=1)[:, :L]
    return (y + uf * dbias.astype(F32)).astype(u.dtype)


def hyena_mixer(p, conv_w, conv_b, w1, b1, freq, w2, b2, w3, decay, dbias):
    S = p.shape[1]
    u = dwconv3(p, conv_w, conv_b)
    v = u[..., :HY_WIDTH]
    x1 = u[..., HY_WIDTH:2 * HY_WIDTH]
    x2 = u[..., 2 * HY_WIDTH:]
    kf = hyena_filters(S, w1, b1, freq, w2, b2, w3, decay)
    z = x1 * fftconv(v, kf[:, 0], dbias[0])
    return x2 * fftconv(z, kf[:, 1], dbias[1])


def axial_gqa(q, k, v, g_q, g_k):
    B, S, _, E = q.shape
    q = rms_norm(q, g_q)
    k = rms_norm(k, g_k)
    ROWS = S // GRID_W
    rows = jnp.repeat(jnp.arange(ROWS), GRID_W)
    cols = jnp.tile(jnp.arange(GRID_W), ROWS)
    q = axial_rope(q, rows, cols)
    k = axial_rope(k, rows, cols)
    nq = S // Q_BLOCK
    qb = q.reshape(B, nq, Q_BLOCK, C_KV_HEADS, C_GROUP, E).transpose(1, 0, 2, 3, 4, 5)
    scale = E ** -0.5

    def attend(qblk):
        s = jnp.einsum('bqgre,bkge->bgrqk', qblk, k).astype(F32) * scale
        pr = jax.nn.softmax(s, axis=-1)
        return jnp.einsum('bgrqk,bkge->bqgre', pr.astype(v.dtype), v)

    o = lax.map(attend, qb)
    return o.transpose(1, 0, 2, 3, 4, 5).reshape(B, S, C_WIDTH)


def conv_geglu(h, w_gate, w_up, conv_w, conv_b, w_down):
    gate = dwconv3(h @ w_gate, conv_w, conv_b)
    return (jax.nn.gelu(gate, approximate=True) * (h @ w_up)) @ w_down


def setup_inputs(seed: int = 0) -> dict:
    key = jax.random.key(seed)
    ks = jax.random.split(key, 32)

    def nrm(k, shape, scale):
        return jax.random.normal(k, shape, F32) * scale

    def gain(k, shape):
        return 1.0 + 0.05 * jax.random.normal(k, shape, F32)

    L_ = DEPTH
    base_decay = jnp.abs(jnp.linspace(math.log(1e-2) / 1.5, math.log(1e-2) / 0.3, HY_WIDTH, dtype=F32))
    hy_decay = base_decay[None, None, None, :] * (1.0 + 0.05 * jax.random.normal(ks[17], (L_, HY_ORDER, 2, HY_WIDTH), F32))
    return {
        "x": jax.random.normal(ks[0], (BATCH, SEQ, D_MODEL), F32),
        "g_mix_pre": gain(ks[1], (L_, D_MODEL)),
        "g_mix_post": gain(ks[2], (L_, D_MODEL)),
        "g_ffn_pre": gain(ks[3], (L_, D_MODEL)),
        "g_ffn_post": gain(ks[4], (L_, D_MODEL)),
        "w_in": nrm(ks[5], (L_, D_MODEL, PROJ_WIDTH), D_MODEL ** -0.5),
        "w_out": nrm(ks[6], (L_, MIX_WIDTH, D_MODEL), MIX_WIDTH ** -0.5),
        "g_q": gain(ks[7], (L_, HEAD_DIM)),
        "g_k": gain(ks[8], (L_, HEAD_DIM)),
        "hy_conv_w": nrm(ks[9], (L_, 3, (HY_ORDER + 1) * HY_WIDTH), 3 ** -0.5),
        "hy_conv_b": nrm(ks[10], (L_, (HY_ORDER + 1) * HY_WIDTH), 0.02),
        "hy_w1": nrm(ks[11], (L_, HY_EMB, HY_HIDDEN), HY_EMB ** -0.5),
        "hy_b1": nrm(ks[12], (L_, HY_HIDDEN), 0.02),
        "hy_freq": gain(ks[13], (L_, 2, HY_HIDDEN)),
        "hy_w2": nrm(ks[14], (L_, HY_HIDDEN, HY_HIDDEN), HY_HIDDEN ** -0.5),
        "hy_b2": nrm(ks[15], (L_, HY_HIDDEN), 0.02),
        "hy_w3": nrm(ks[16], (L_, HY_HIDDEN, HY_ORDER * 2 * HY_WIDTH), HY_HIDDEN ** -0.5),
        "hy_decay": hy_decay,
        "hy_d": nrm(ks[18], (L_, HY_ORDER, HY_WIDTH), 0.1),
        "ffn_w_gate": nrm(ks[19], (L_, D_MODEL, D_FF), D_MODEL ** -0.5),
        "ffn_w_up": nrm(ks[20], (L_, D_MODEL, D_FF), D_MODEL ** -0.5),
        "ffn_conv_w": nrm(ks[21], (L_, 3, D_FF), 3 ** -0.5),
        "ffn_conv_b": nrm(ks[22], (L_, D_FF), 0.02),
        "ffn_w_down": nrm(ks[23], (L_, D_FF, D_MODEL), D_FF ** -0.5),
    }


def reference(x, g_mix_pre, g_mix_post, g_ffn_pre, g_ffn_post, w_in, w_out, g_q, g_k,
              hy_conv_w, hy_conv_b, hy_w1, hy_b1, hy_freq, hy_w2, hy_b2, hy_w3, hy_decay, hy_d,
              ffn_w_gate, ffn_w_up, ffn_conv_w, ffn_conv_b, ffn_w_down):
    B, S, _ = x.shape
    cos1, sin1 = rope_angles(jnp.arange(S), HEAD_DIM)
    for i in range(DEPTH):
        h = rms_norm(x, g_mix_pre[i])
        p = h @ w_in[i]
        qa = apply_rope(p[..., A_Q0:A_K0].reshape(B, S, A_HEADS, HEAD_DIM), cos1, sin1)
        ka = apply_rope(p[..., A_K0:A_V0].reshape(B, S, A_HEADS, HEAD_DIM), cos1, sin1)
        va = p[..., A_V0:HY_0].reshape(B, S, A_HEADS, HEAD_DIM)
        out_a = dilated_attention(qa, ka, va).reshape(B, S, A_WIDTH)
        out_b = hyena_mixer(p[..., HY_0:C_Q0], hy_conv_w[i], hy_conv_b[i], hy_w1[i], hy_b1[i],
                            hy_freq[i], hy_w2[i], hy_b2[i], hy_w3[i], hy_decay[i], hy_d[i])
        qc = p[..., C_Q0:C_K0].reshape(B, S, C_Q_HEADS, HEAD_DIM)
        kc = p[..., C_K0:C_V0].reshape(B, S, C_KV_HEADS, HEAD_DIM)
        vc = p[..., C_V0:PROJ_WIDTH].reshape(B, S, C_KV_HEADS, HEAD_DIM)
        out_c = axial_gqa(qc, kc, vc, g_q[i], g_k[i])
        mix = jnp.concatenate([out_a, out_b, out_c], axis=-1) @ w_out[i]
        x = x + rms_norm(mix, g_mix_post[i])
        h = rms_norm(x, g_ffn_pre[i])
        f = conv_geglu(h, ffn_w_gate[i], ffn_w_up[i], ffn_conv_w[i], ffn_conv_b[i], ffn_w_down[i])
        x = x + rms_norm(f, g_ffn_post[i])
    return x
```

```python
import functools
import math

import numpy as np
import jax
import jax.numpy as jnp
from jax import lax
from jax.experimental import pallas as pl
from jax.experimental.pallas import tpu as pltpu

F32 = jnp.float32
BF16 = jnp.bfloat16

D_MODEL = 1024
SEQ = 2048
DEPTH = 2
HEAD_DIM = 64
A_WIDTH = 384
HY_WIDTH = 256
C_WIDTH = 384
C_KV_WIDTH = 128
PROJ_WIDTH = 2560
GRID_W = 64
ROPE_THETA = 10000.0
HY_BANDS = 16
HY_EMB = 33
HY_HIDDEN = 64
D_FF = 2816
EPS = 1e-6
N_SIDE = 64
SCALE = HEAD_DIM ** -0.5
NEG = -1e30

LANES = 128
VMEM_LIMIT = 56 * 1024 * 1024

HIGHEST = lax.Precision.HIGHEST


def _params(n_grid_dims=1):
    return pltpu.CompilerParams(
        dimension_semantics=("arbitrary",) * n_grid_dims, vmem_limit_bytes=VMEM_LIMIT)


def _resident(shape):
    nd = len(shape)
    return pl.BlockSpec(shape, lambda *_: (0,) * nd, pipeline_mode=pl.Buffered(1))


def _rope_tables():
    def angles(pos, dim):
        freqs = ROPE_THETA ** (-np.arange(0, dim, 2, dtype=np.float64) / dim)
        ang = pos.astype(np.float64)[:, None] * freqs[None, :]
        return np.cos(ang), np.sin(ang)

    pos = np.arange(SEQ)
    c, s = angles(pos, HEAD_DIM)
    cos_a = np.tile(np.concatenate([c, c], -1), (1, 2))
    sin_a = np.tile(np.concatenate([-s, s], -1), (1, 2))
    cr, sr = angles(pos // GRID_W, HEAD_DIM // 2)
    cc, sc = angles(pos % GRID_W, HEAD_DIM // 2)
    cos_c = np.tile(np.concatenate([cr, cr, cc, cc], -1), (1, 2))
    sin_c = np.tile(np.concatenate([-sr, sr, -sc, sc], -1), (1, 2))
    return tuple(jnp.asarray(t, F32) for t in (cos_a, sin_a, cos_c, sin_c))


def _head_mean_matrix():
    m = np.kron(np.eye(LANES // HEAD_DIM), np.full((HEAD_DIM, HEAD_DIM), 1.0 / HEAD_DIM))
    return jnp.asarray(m, BF16)


def _hyena_positions():
    L = SEQ
    t = np.linspace(0.0, 1.0, L)
    bands = np.linspace(1e-4, HY_BANDS - 1, HY_BANDS)
    ang = 2.0 * math.pi * bands[None, :] * np.arange(L)[:, None] / L
    z = np.concatenate([t[:, None], np.cos(ang), -np.sin(ang)], -1)
    zt = np.zeros((LANES, 2 * L))
    zt[:HY_EMB, :L] = z[::-1].T
    zt[:HY_EMB, L:] = z.T
    trow = np.concatenate([t[::-1], t])[None, :]
    return jnp.asarray(zt, F32), jnp.asarray(trow, F32)


def _rope(x, cos, sin_signed, half, lane):
    first = (lane & (2 * half - 1)) < half
    swapped = jnp.where(first, pltpu.roll(x, LANES - half, 1), pltpu.roll(x, half, 1))
    return x * cos + swapped * sin_signed


def _rms(x, gain):
    ms = jnp.mean(x * x, axis=-1, keepdims=True)
    return x * lax.rsqrt(ms + EPS) * gain


IN_TM = 1024
IN_RB = 256
_IN_CHUNKS = ((0, 768), (768, 1280), (1280, 1792), (1792, 2304), (2304, 2560))


def _inproj_kernel(x_ref, g_ref, w_ref, cosa_ref, sina_ref, cosc_ref, sinc_ref, gq_ref, gk_ref, bd_ref,
                   qa_ref, ka_ref, va_ref, hy_ref, qc_ref, kc_ref, vc_ref):
    lane = lax.broadcasted_iota(jnp.int32, (IN_RB, LANES), 1)

    def head_norm(v, gain):
        sq = v * v
        hi = sq.astype(BF16)
        lo = (sq - hi.astype(F32)).astype(BF16)
        ms = (jnp.dot(hi, bd_ref[...], preferred_element_type=F32)
              + jnp.dot(lo, bd_ref[...], preferred_element_type=F32))
        return v * lax.rsqrt(ms + EPS) * gain

    def sub(i, carry):
        rows = pl.ds(pl.multiple_of(i * IN_RB, IN_RB), IN_RB)
        h = _rms(x_ref[rows, :], g_ref[...]).astype(BF16)
        cosa, sina = cosa_ref[rows, :], sina_ref[rows, :]
        cosc, sinc = cosc_ref[rows, :], sinc_ref[rows, :]
        for c0, c1 in _IN_CHUNKS:
            p = jnp.dot(h, w_ref[:, c0:c1], preferred_element_type=F32)
            for b in range((c1 - c0) // LANES):
                blk = p[:, b * LANES:(b + 1) * LANES]
                gb = c0 // LANES + b
                if gb < 3:
                    cols = slice(gb * LANES, (gb + 1) * LANES)
                    qa_ref[rows, cols] = (_rope(blk, cosa, sina, 32, lane) * SCALE).astype(BF16)
                elif gb < 6:
                    cols = slice((gb - 3) * LANES, (gb - 2) * LANES)
                    ka_ref[rows, cols] = _rope(blk, cosa, sina, 32, lane).astype(BF16)
                elif gb < 9:
                    cols = slice((gb - 6) * LANES, (gb - 5) * LANES)
                    va_ref[rows, cols] = blk.astype(BF16)
                elif gb < 15:
                    cols = slice((gb - 9) * LANES, (gb - 8) * LANES)
                    hy_ref[rows, cols] = blk.astype(BF16)
                elif gb < 18:
                    cols = slice((gb - 15) * LANES, (gb - 14) * LANES)
                    q = _rope(head_norm(blk, gq_ref[...]), cosc, sinc, 16, lane)
                    qc_ref[rows, cols] = (q * SCALE).astype(BF16)
                elif gb == 18:
                    kc_ref[rows, :] = _rope(head_norm(blk, gk_ref[...]), cosc, sinc, 16, lane).astype(BF16)
                else:
                    vc_ref[rows, :] = blk.astype(BF16)
        return carry

    lax.fori_loop(0, IN_TM // IN_RB, sub, 0)


def _inproj(x2, g, w, tables, gq, gk, bd):
    n = x2.shape[0]
    tiles_per_seq = SEQ // IN_TM
    row = lambda width: pl.BlockSpec((IN_TM, width), lambda i: (i, 0))
    tab = pl.BlockSpec((IN_TM, LANES), lambda i: (i % tiles_per_seq, 0))
    outs = [A_WIDTH, A_WIDTH, A_WIDTH, 3 * HY_WIDTH, C_WIDTH, C_KV_WIDTH, C_KV_WIDTH]
    return pl.pallas_call(
        _inproj_kernel,
        grid=(n // IN_TM,),
        in_specs=[row(D_MODEL), _resident((1, D_MODEL)), _resident((D_MODEL, PROJ_WIDTH)),
                  tab, tab, tab, tab, _resident((1, LANES)), _resident((1, LANES)), _resident((LANES, LANES))],
        out_specs=[row(wd) for wd in outs],
        out_shape=[jax.ShapeDtypeStruct((n, wd), BF16) for wd in outs],
        compiler_params=_params(),
        name="inproj",
    )(x2, g, w, *tables, gq, gk, bd)


DL_BQ = 128
DL_BK = 256


def _band_problem(q, k, v, mask, lane):
    lo = lax.broadcasted_iota(jnp.int32, (1, LANES), 1) < HEAD_DIM
    outs = []
    lse = jnp.zeros((q.shape[0], LANES), F32)
    for j in range(A_WIDTH // LANES):
        cols = slice(j * LANES, (j + 1) * LANES)
        qj, kj, vj = q[:, cols], k[:, cols], v[:, cols]
        acc = None
        for g in range(2):
            sel = lo if g == 0 else jnp.logical_not(lo)
            qm = jnp.where(sel, qj, jnp.zeros_like(qj))
            s = lax.dot_general(qm, kj, (((1,), (1,)), ((), ())), preferred_element_type=F32)
            s = jnp.where(mask, s, NEG)
            m = jnp.max(s, axis=-1, keepdims=True)
            p = jnp.exp(s - m)
            l = jnp.sum(p, axis=-1, keepdims=True)
            pn = (p * (1.0 / l)).astype(BF16)
            vm = jnp.where(sel, vj, jnp.zeros_like(vj))
            d = jnp.dot(pn, vm, preferred_element_type=F32)
            acc = d if acc is None else acc + d
            lse = jnp.where(lane == 2 * j + g, m + jnp.log(l), lse)
        outs.append(acc)
    return jnp.concatenate(outs, axis=1), lse


def _dilated_kernel(q1, k1, v1, q4, k4, v4, q16, k16, v16, o1, l1, o4, l4, o16, l16):
    lane = lax.broadcasted_iota(jnp.int32, (DL_BQ, LANES), 1)
    row = lax.broadcasted_iota(jnp.int32, (DL_BQ, DL_BK), 0)
    col = lax.broadcasted_iota(jnp.int32, (DL_BQ, DL_BK), 1)

    def banded(qr, kr, vr, orf, lrf, length, lane0, lse0):
        qcols = pl.ds(lane0, A_WIDTH)
        lcols = pl.ds(lse0, LANES)

        def blk(i, carry):
            t0 = pl.multiple_of(i * DL_BQ, DL_BQ)
            ks = pl.multiple_of(jnp.clip(t0 - N_SIDE, 0, length - DL_BK), N_SIDE)
            diff = col - row + (ks - t0)
            mask = (diff <= N_SIDE) & (diff >= -N_SIDE)
            o, lse = _band_problem(qr[pl.ds(t0, DL_BQ), qcols], kr[pl.ds(ks, DL_BK), qcols],
                                   vr[pl.ds(ks, DL_BK), qcols], mask, lane)
            orf[pl.ds(t0, DL_BQ), qcols] = o.astype(BF16)
            lrf[pl.ds(t0, DL_BQ), lcols] = lse
            return carry

        lax.fori_loop(0, length // DL_BQ, blk, 0)

    banded(q1, k1, v1, o1, l1, SEQ, 0, 0)
    for r in range(4):
        banded(q4, k4, v4, o4, l4, SEQ // 4, r * A_WIDTH, r * LANES)

    d16 = (lax.broadcasted_iota(jnp.int32, (DL_BQ, DL_BQ), 1)
           - lax.broadcasted_iota(jnp.int32, (DL_BQ, DL_BQ), 0))
    mask16 = (d16 <= N_SIDE) & (d16 >= -N_SIDE)

    def res16(r, carry):
        qcols = pl.ds(pl.multiple_of(r * A_WIDTH, LANES), A_WIDTH)
        lcols = pl.ds(pl.multiple_of(r * LANES, LANES), LANES)
        o, lse = _band_problem(q16[:, qcols], k16[:, qcols], v16[:, qcols], mask16, lane)
        o16[:, qcols] = o.astype(BF16)
        l16[:, lcols] = lse
        return carry

    lax.fori_loop(0, 16, res16, 0)


def _dilated(qa, ka, va, batch):
    def views(a, width):
        return (a.reshape(batch, SEQ, width), a.reshape(batch, SEQ // 4, 4 * width),
                a.reshape(batch, SEQ // 16, 16 * width))

    def specs(width):
        return (pl.BlockSpec((None, SEQ, width), lambda b: (b, 0, 0)),
                pl.BlockSpec((None, SEQ // 4, 4 * width), lambda b: (b, 0, 0)),
                pl.BlockSpec((None, SEQ // 16, 16 * width), lambda b: (b, 0, 0)))

    q, k, v = views(qa, A_WIDTH), views(ka, A_WIDTH), views(va, A_WIDTH)
    sa, sl = specs(A_WIDTH), specs(LANES)
    in_specs = [sa[d] for d in range(3) for _ in range(3)]
    args = [t[d] for d in range(3) for t in (q, k, v)]
    out_specs, out_shape = [], []
    for d, rows in enumerate((SEQ, SEQ // 4, SEQ // 16)):
        mult = SEQ // rows
        out_specs += [sa[d], sl[d]]
        out_shape += [jax.ShapeDtypeStruct((batch, rows, mult * A_WIDTH), BF16),
                      jax.ShapeDtypeStruct((batch, rows, mult * LANES), F32)]
    o1, l1, o4, l4, o16, l16 = pl.pallas_call(
        _dilated_kernel, grid=(batch,), in_specs=in_specs, out_specs=out_specs, out_shape=out_shape,
        compiler_params=_params(), name="dilated",
    )(*args)
    n = batch * SEQ
    return ([o.reshape(n, A_WIDTH) for o in (o1, o4, o16)], [l.reshape(n, LANES) for l in (l1, l4, l16)])


GQ_TQ = 256


def _gqa_kernel(q_ref, k_ref, v_ref, o_ref, ksw_ref, vm_ref):
    lane = lax.broadcasted_iota(jnp.int32, (SEQ, LANES), 1)
    lo = lane < HEAD_DIM
    k = k_ref[...]
    v = v_ref[...]
    ksw_ref[...] = pltpu.roll(k.astype(F32), HEAD_DIM, 1).astype(BF16)
    vsw = pltpu.roll(v.astype(F32), HEAD_DIM, 1).astype(BF16)
    zero = jnp.zeros_like(v)
    vm_ref[0] = jnp.where(lo, v, zero)
    vm_ref[1] = jnp.where(lo, zero, vsw)
    vm_ref[2] = jnp.where(lo, vsw, zero)
    vm_ref[3] = jnp.where(lo, zero, v)
    lo_q = lax.broadcasted_iota(jnp.int32, (1, LANES), 1) < HEAD_DIM

    def blk(i, carry):
        rows = pl.ds(pl.multiple_of(i * GQ_TQ, GQ_TQ), GQ_TQ)
        for j in range(C_WIDTH // LANES):
            cols = slice(j * LANES, (j + 1) * LANES)
            qj = q_ref[rows, cols]
            acc = None
            for half in range(2):
                g = (2 * j + half) // 3
                sel = lo_q if half == 0 else jnp.logical_not(lo_q)
                qm = jnp.where(sel, qj, jnp.zeros_like(qj))
                kk = k_ref[...] if g == half else ksw_ref[...]
                s = lax.dot_general(qm, kk, (((1,), (1,)), ((), ())), preferred_element_type=F32)
                m = jnp.max(s, axis=-1, keepdims=True)
                p = jnp.exp(s - m)
                l = jnp.sum(p, axis=-1, keepdims=True)
                pn = (p * (1.0 / l)).astype(BF16)
                d = jnp.dot(pn, vm_ref[2 * g + half], preferred_element_type=F32)
                acc = d if acc is None else acc + d
            o_ref[rows, cols] = acc.astype(BF16)
        return carry

    lax.fori_loop(0, SEQ // GQ_TQ, blk, 0)


def _gqa(qc, kc, vc, batch):
    spec = lambda width: pl.BlockSpec((None, SEQ, width), lambda b: (b, 0, 0))
    out = pl.pallas_call(
        _gqa_kernel, grid=(batch,),
        in_specs=[spec(C_WIDTH), spec(C_KV_WIDTH), spec(C_KV_WIDTH)],
        out_specs=spec(C_WIDTH),
        out_shape=jax.ShapeDtypeStruct((batch, SEQ, C_WIDTH), BF16),
        scratch_shapes=[pltpu.VMEM((SEQ, LANES), BF16), pltpu.VMEM((4, SEQ, LANES), BF16)],
        compiler_params=_params(), name="gqa",
    )(qc.reshape(batch, SEQ, C_WIDTH), kc.reshape(batch, SEQ, C_KV_WIDTH), vc.reshape(batch, SEQ, C_KV_WIDTH))
    return out.reshape(batch * SEQ, C_WIDTH)


def _filter_kernel(z_ref, trow_ref, w1t_ref, b1_ref, f0_ref, w2t_ref, b2_ref, f1_ref, w3t_ref, dec_ref, out_ref):
    L = SEQ
    dot = functools.partial(jnp.dot, precision=HIGHEST, preferred_element_type=F32)
    h = jnp.sin(f0_ref[...] * (dot(w1t_ref[...], z_ref[...]) + b1_ref[...]))
    h = jnp.sin(f1_ref[...] * (dot(w2t_ref[...], h) + b2_ref[...]))
    w3t = w3t_ref[...]
    dec = dec_ref[...]
    trow = trow_ref[...]
    hf = dot(w3t[:HY_WIDTH], h[:, L:]) * jnp.exp(-trow[:, L:] * dec[:HY_WIDTH])
    hb = dot(w3t[HY_WIDTH:], h[:, :L]) * jnp.exp(-trow[:, :L] * dec[HY_WIDTH:])
    col = lax.broadcasted_iota(jnp.int32, hb.shape, 1)
    hb = jnp.where(col == 0, 0.0, pltpu.roll(hb, 1, 1))
    norm = (jnp.sum(jnp.abs(hf), axis=-1, keepdims=True) + jnp.sum(jnp.abs(hb), axis=-1, keepdims=True))
    out_ref[:, :L] = hb / norm
    out_ref[:, L:] = hf / norm


def _filters(zt, trow, w1, b1, freq, w2, b2, w3, decay):
    colv = lambda a: a.reshape(-1, 1).astype(F32)
    w1t = jnp.zeros((HY_HIDDEN, LANES), F32).at[:, :HY_EMB].set(w1.T)
    order = pl.BlockSpec((2 * HY_WIDTH, HY_HIDDEN), lambda o: (o, 0))
    return pl.pallas_call(
        _filter_kernel, grid=(2,),
        in_specs=[_resident(zt.shape), _resident(trow.shape), _resident(w1t.shape), _resident((HY_HIDDEN, 1)),
                  _resident((HY_HIDDEN, 1)), _resident((HY_HIDDEN, HY_HIDDEN)), _resident((HY_HIDDEN, 1)),
                  _resident((HY_HIDDEN, 1)), order, pl.BlockSpec((2 * HY_WIDTH, 1), lambda o: (o, 0))],
        out_specs=pl.BlockSpec((HY_WIDTH, 2 * SEQ), lambda o: (o, 0)),
        out_shape=jax.ShapeDtypeStruct((2 * HY_WIDTH, 2 * SEQ), F32),
        compiler_params=_params(), name="filters",
    )(zt, trow, w1t, colv(b1), colv(freq[0]), w2.T, colv(b2), colv(freq[1]), w3.T, colv(decay))


HY_TB = 256
HY_NB = SEQ // HY_TB


def _hyena_kernel(cw_ref, cb_ref, d_ref, pv_ref, px1_ref, px2_ref, g0_ref, g1_ref, o_ref,
                  gsh_ref, ust_ref, acc_ref):
    c = pl.program_id(0)
    nb = pv_ref.shape[0]
    col = lax.broadcasted_iota(jnp.int32, (nb, SEQ), 1)

    def dwconv(p_ref, ch):
        p = p_ref[...].astype(F32)
        prev = jnp.where(col == 0, 0.0, pltpu.roll(p, 1, 1))
        nxt = jnp.where(col == SEQ - 1, 0.0, pltpu.roll(p, SEQ - 1, 1))
        return prev * cw_ref[0, ch] + p * cw_ref[1, ch] + nxt * cw_ref[2, ch] + cb_ref[ch]

    def longconv(u, g_ref, skip):
        g = jnp.broadcast_to(g_ref[...], (LANES, 2 * SEQ))
        gsh_ref[...] = pltpu.roll(g, 0, 1, stride=1, stride_axis=0).astype(BF16)
        ub = u.astype(BF16)
        for tb in range(HY_NB):
            ust_ref[tb * nb:(tb + 1) * nb, :] = ub[:, tb * HY_TB:(tb + 1) * HY_TB]
        acc_ref[...] = jnp.zeros_like(acc_ref)
        for dl in range(-(HY_NB - 1), HY_NB):
            x0 = HY_TB * (dl + HY_NB)
            rhs = jnp.concatenate([gsh_ref[:, x0:x0 + HY_TB], gsh_ref[:, x0 - LANES:x0 - LANES + HY_TB]], axis=0)
            b0, b1 = max(0, -dl), min(HY_NB, HY_NB - dl)
            acc_ref[(b0 + dl) * nb:(b1 + dl) * nb, :] += jnp.dot(
                ust_ref[b0 * nb:b1 * nb, :], rhs, preferred_element_type=F32)
        y = jnp.concatenate([acc_ref[tb * nb:(tb + 1) * nb, :] for tb in range(HY_NB)], axis=1)
        return y + skip * u

    v = dwconv(pv_ref, c)
    x1 = dwconv(px1_ref, HY_WIDTH + c)
    x2 = dwconv(px2_ref, 2 * HY_WIDTH + c)
    z = x1 * longconv(v, g0_ref, d_ref[0, c])
    o_ref[...] = (x2 * longconv(z, g1_ref, d_ref[1, c])).astype(BF16)


def _hyena(pt, gt, conv_w, conv_b, dbias):
    nb = pt.shape[1]
    smem = pl.BlockSpec(memory_space=pltpu.SMEM)
    chan = lambda off: pl.BlockSpec((None, nb, SEQ), lambda c: (c + off, 0, 0))
    filt = lambda off: pl.BlockSpec((None, 1, 2 * SEQ), lambda c: (c + off, 0, 0))
    gt3 = gt.reshape(2 * HY_WIDTH, 1, 2 * SEQ)
    return pl.pallas_call(
        _hyena_kernel, grid=(HY_WIDTH,),
        in_specs=[smem, smem, smem, chan(0), chan(HY_WIDTH), chan(2 * HY_WIDTH), filt(0), filt(HY_WIDTH)],
        out_specs=pl.BlockSpec((None, nb, SEQ), lambda c: (c, 0, 0)),
        out_shape=jax.ShapeDtypeStruct((HY_WIDTH, nb, SEQ), BF16),
        scratch_shapes=[pltpu.VMEM((LANES, 2 * SEQ), BF16), pltpu.VMEM((HY_NB * nb, HY_TB), BF16),
                        pltpu.VMEM((HY_NB * nb, HY_TB), F32)],
        compiler_params=_params(), name="hyena",
    )(conv_w, conv_b, dbias, pt, pt, pt, gt3, gt3)


OP_TM = 1024
OP_RB = 256


def _outproj_kernel(o1_ref, o4_ref, o16_ref, l1_ref, l4_ref, l16_ref, hy_ref, oc_ref, w_ref, x_ref, g_ref, out_ref):
    lane = lax.broadcasted_iota(jnp.int32, (OP_RB, LANES), 1)
    lo = lane < HEAD_DIM

    def sub(i, carry):
        rows = pl.ds(pl.multiple_of(i * OP_RB, OP_RB), OP_RB)
        lses = [r[rows, :] for r in (l1_ref, l4_ref, l16_ref)]
        mx = jnp.maximum(jnp.maximum(lses[0], lses[1]), lses[2])
        es = [jnp.exp(l - mx) for l in lses]
        inv = 1.0 / (es[0] + es[1] + es[2])
        ws = [e * inv for e in es]
        mix = None
        for j in range(A_WIDTH // LANES):
            cols = slice(j * LANES, (j + 1) * LANES)
            a = None
            for w, o_ref in zip(ws, (o1_ref, o4_ref, o16_ref)):
                wfull = jnp.where(lo, jnp.broadcast_to(w[:, 2 * j:2 * j + 1], (OP_RB, LANES)),
                                  jnp.broadcast_to(w[:, 2 * j + 1:2 * j + 2], (OP_RB, LANES)))
                t = wfull * o_ref[rows, cols].astype(F32)
                a = t if a is None else a + t
            d = jnp.dot(a.astype(BF16), w_ref[cols, :], preferred_element_type=F32)
            mix = d if mix is None else mix + d
        mix += jnp.dot(hy_ref[rows, :], w_ref[A_WIDTH:A_WIDTH + HY_WIDTH, :], preferred_element_type=F32)
        mix += jnp.dot(oc_ref[rows, :], w_ref[A_WIDTH + HY_WIDTH:, :], preferred_element_type=F32)
        out_ref[rows, :] = x_ref[rows, :] + _rms(mix, g_ref[...])
        return carry

    lax.fori_loop(0, OP_TM // OP_RB, sub, 0)


def _outproj(os_, ls_, hyo, oc, w, x2, g):
    n = x2.shape[0]
    row = lambda width: pl.BlockSpec((OP_TM, width), lambda i: (i, 0))
    return pl.pallas_call(
        _outproj_kernel, grid=(n // OP_TM,),
        in_specs=[row(A_WIDTH)] * 3 + [row(LANES)] * 3 + [row(HY_WIDTH), row(C_WIDTH),
                  _resident((D_MODEL, D_MODEL)), row(D_MODEL), _resident((1, D_MODEL))],
        out_specs=row(D_MODEL),
        out_shape=jax.ShapeDtypeStruct((n, D_MODEL), F32),
        compiler_params=_params(), name="outproj",
    )(*os_, *ls_, hyo, oc, w, x2, g)


FF_TM = 512
FF_RB = 256
FF_HALO = 16


def _ffn_kernel(x_ref, xp_ref, xn_ref, gpre_ref, wg_ref, wu_ref, cw_ref, cb_ref, wd_ref, gpost_ref, out_ref, xs_ref):
    i = pl.program_id(0)
    tiles_per_seq = SEQ // FF_TM
    first = (i % tiles_per_seq) == 0
    last = (i % tiles_per_seq) == tiles_per_seq - 1
    xs_ref[:FF_HALO, :] = jnp.where(first, 0.0, xp_ref[...])
    xs_ref[FF_HALO:FF_HALO + FF_TM, :] = x_ref[...]
    xs_ref[FF_HALO + FF_TM:, :] = jnp.where(last, 0.0, xn_ref[...])

    def sub(s, carry):
        r0 = pl.multiple_of(s * FF_RB, FF_RB)
        xh = xs_ref[pl.ds(r0, FF_RB + 2 * FF_HALO), :]
        h = _rms(xh, gpre_ref[...]).astype(BF16)
        gate = jnp.dot(h, wg_ref[...], preferred_element_type=F32)
        up = jnp.dot(h[FF_HALO:FF_HALO + FF_RB], wu_ref[...], preferred_element_type=F32)
        gc = (gate[FF_HALO - 1:FF_HALO - 1 + FF_RB] * cw_ref[0:1, :]
              + gate[FF_HALO:FF_HALO + FF_RB] * cw_ref[1:2, :]
              + gate[FF_HALO + 1:FF_HALO + 1 + FF_RB] * cw_ref[2:3, :] + cb_ref[...])
        act = (jax.nn.gelu(gc, approximate=True) * up).astype(BF16)
        f = jnp.dot(act, wd_ref[...], preferred_element_type=F32)
        x = xh[FF_HALO:FF_HALO + FF_RB]
        out_ref[pl.ds(r0, FF_RB), :] = x + _rms(f, gpost_ref[...])
        return carry

    lax.fori_loop(0, FF_TM // FF_RB, sub, 0)


def _ffn(x2, gpre, wg, wu, cw, cb, wd, gpost):
    n = x2.shape[0]
    per = FF_TM // FF_HALO
    last_blk = n // FF_HALO - 1
    return pl.pallas_call(
        _ffn_kernel, grid=(n // FF_TM,),
        in_specs=[pl.BlockSpec((FF_TM, D_MODEL), lambda i: (i, 0)),
                  pl.BlockSpec((FF_HALO, D_MODEL), lambda i: (jnp.maximum(i * per - 1, 0), 0)),
                  pl.BlockSpec((FF_HALO, D_MODEL), lambda i: (jnp.minimum((i + 1) * per, last_blk), 0)),
                  _resident((1, D_MODEL)), _resident((D_MODEL, D_FF)), _resident((D_MODEL, D_FF)),
                  _resident((3, D_FF)), _resident((1, D_FF)), _resident((D_FF, D_MODEL)), _resident((1, D_MODEL))],
        out_specs=pl.BlockSpec((FF_TM, D_MODEL), lambda i: (i, 0)),
        out_shape=jax.ShapeDtypeStruct((n, D_MODEL), F32),
        scratch_shapes=[pltpu.VMEM((FF_TM + 2 * FF_HALO, D_MODEL), F32)],
        compiler_params=_params(), name="ffn",
    )(x2, x2, x2, gpre, wg, wu, cw, cb, wd, gpost)


def kernel(x, g_mix_pre, g_mix_post, g_ffn_pre, g_ffn_post, w_in, w_out, g_q, g_k, hy_conv_w, hy_conv_b, hy_w1, hy_b1, hy_freq, hy_w2, hy_b2, hy_w3, hy_decay, hy_d, ffn_w_gate, ffn_w_up, ffn_conv_w, ffn_conv_b, ffn_w_down):
    batch = x.shape[0]
    n = batch * SEQ
    tables = _rope_tables()
    bd = _head_mean_matrix()
    zt, trow = _hyena_positions()
    rowv = lambda a: a.reshape(1, -1).astype(F32)
    x2 = x.reshape(n, D_MODEL)
    for i in range(DEPTH):
        qa, ka, va, hy, qc, kc, vc = _inproj(
            x2, rowv(g_mix_pre[i]), w_in[i].astype(BF16), tables,
            rowv(jnp.tile(g_q[i], 2)), rowv(jnp.tile(g_k[i], 2)), bd)
        os_, ls_ = _dilated(qa, ka, va, batch)
        oc = _gqa(qc, kc, vc, batch)
        gt = _filters(zt, trow, hy_w1[i], hy_b1[i], hy_freq[i], hy_w2[i], hy_b2[i], hy_w3[i], hy_decay[i])
        pt = jnp.transpose(hy.reshape(batch, SEQ, 3 * HY_WIDTH), (2, 0, 1))
        ot = _hyena(pt, gt, hy_conv_w[i].astype(F32), hy_conv_b[i].astype(F32), hy_d[i].astype(F32))
        hyo = jnp.transpose(ot, (1, 2, 0)).reshape(n, HY_WIDTH)
        x2 = _outproj(os_, ls_, hyo, oc, w_out[i].astype(BF16), x2, rowv(g_mix_post[i]))
        x2 = _ffn(x2, rowv(g_ffn_pre[i]), ffn_w_gate[i].astype(BF16), ffn_w_up[i].astype(BF16),
                  ffn_conv_w[i].astype(F32), rowv(ffn_conv_b[i]), ffn_w_down[i].astype(BF16),
                  rowv(g_ffn_post[i]))
    return x2.reshape(batch, SEQ, D_MODEL)
```

```python
import functools
import math

import numpy as np
import jax
import jax.numpy as jnp
from jax import lax
from jax.experimental import pallas as pl
from jax.experimental.pallas import tpu as pltpu

F32 = jnp.float32
BF16 = jnp.bfloat16

D_MODEL = 1024
SEQ = 2048
DEPTH = 2
HEAD_DIM = 64
A_WIDTH = 384
HY_WIDTH = 256
C_WIDTH = 384
C_KV_WIDTH = 128
PROJ_WIDTH = 2560
GRID_W = 64
ROPE_THETA = 10000.0
HY_BANDS = 16
HY_EMB = 33
HY_HIDDEN = 64
D_FF = 2816
EPS = 1e-6
N_SIDE = 64
QSCALE = HEAD_DIM ** -0.5 * math.log2(math.e)
NEG = -1e30

LANES = 128
VMEM_LIMIT = 56 * 1024 * 1024

HIGHEST = lax.Precision.HIGHEST


def _params(n_grid_dims=1):
    return pltpu.CompilerParams(
        dimension_semantics=("arbitrary",) * n_grid_dims, vmem_limit_bytes=VMEM_LIMIT)


def _resident(shape):
    nd = len(shape)
    return pl.BlockSpec(shape, lambda *_: (0,) * nd, pipeline_mode=pl.Buffered(1))


def _rope_tables():
    def angles(pos, dim):
        freqs = ROPE_THETA ** (-np.arange(0, dim, 2, dtype=np.float64) / dim)
        ang = pos.astype(np.float64)[:, None] * freqs[None, :]
        return np.cos(ang), np.sin(ang)

    pos = np.arange(SEQ)
    c, s = angles(pos, HEAD_DIM)
    cos_a = np.tile(np.concatenate([c, c], -1), (1, 2))
    sin_a = np.tile(np.concatenate([-s, s], -1), (1, 2))
    cr, sr = angles(pos // GRID_W, HEAD_DIM // 2)
    cc, sc = angles(pos % GRID_W, HEAD_DIM // 2)
    cos_c = np.tile(np.concatenate([cr, cr, cc, cc], -1), (1, 2))
    sin_c = np.tile(np.concatenate([-sr, sr, -sc, sc], -1), (1, 2))
    return tuple(jnp.asarray(t, F32) for t in (cos_a, sin_a, cos_c, sin_c))


def _head_mean_matrix():
    m = np.kron(np.eye(LANES // HEAD_DIM), np.full((HEAD_DIM, HEAD_DIM), 1.0 / HEAD_DIM))
    return jnp.asarray(m, BF16)


def _hyena_positions():
    L = SEQ
    t = np.linspace(0.0, 1.0, L)
    bands = np.linspace(1e-4, HY_BANDS - 1, HY_BANDS)
    ang = 2.0 * math.pi * bands[None, :] * np.arange(L)[:, None] / L
    z = np.concatenate([t[:, None], np.cos(ang), -np.sin(ang)], -1)
    zt = np.zeros((LANES, 2 * L))
    zt[:HY_EMB, :L] = z[::-1].T
    zt[:HY_EMB, L:] = z.T
    trow = np.concatenate([t[::-1], t])[None, :]
    return jnp.asarray(zt, F32), jnp.asarray(trow, F32)


def _rope(x, cos, sin_signed, half, lane):
    first = (lane & (2 * half - 1)) < half
    swapped = jnp.where(first, pltpu.roll(x, LANES - half, 1), pltpu.roll(x, half, 1))
    return x * cos + swapped * sin_signed


def _rms(x, gain):
    ms = jnp.mean(x * x, axis=-1, keepdims=True)
    return x * lax.rsqrt(ms + EPS) * gain


IN_TM = 1024
IN_RB = 512
IN_STEPS = IN_TM // IN_RB
_IN_CHUNKS = ((0, 768), (768, 1280), (1280, 1792), (1792, 2304), (2304, 2560))


def _inproj_kernel(x_ref, g_ref, w_ref, cosa_ref, sina_ref, cosc_ref, sinc_ref, gq_ref, gk_ref, bd_ref,
                   qa_ref, ka_ref, va_ref, hy_ref, qc_ref, kc_ref, vc_ref, h_ref):
    lane = lax.broadcasted_iota(jnp.int32, (IN_RB, LANES), 1)

    def normed(i):
        rows = pl.ds(pl.multiple_of(i * IN_RB, IN_RB), IN_RB)
        return _rms(x_ref[rows, :], g_ref[...]).astype(BF16)

    h_ref[0] = normed(0)

    def head_norm(v, gain):
        sq = v * v
        hi = sq.astype(BF16)
        lo = (sq - hi.astype(F32)).astype(BF16)
        ms = (jnp.dot(hi, bd_ref[...], preferred_element_type=F32)
              + jnp.dot(lo, bd_ref[...], preferred_element_type=F32))
        return v * lax.rsqrt(ms + EPS) * gain

    def sub(i, carry):
        rows = pl.ds(pl.multiple_of(i * IN_RB, IN_RB), IN_RB)
        slot = i % 2
        h = h_ref[slot]
        h_ref[1 - slot] = normed(jnp.minimum(i + 1, IN_STEPS - 1))
        cosa, sina = cosa_ref[rows, :], sina_ref[rows, :]
        cosc, sinc = cosc_ref[rows, :], sinc_ref[rows, :]
        for c0, c1 in _IN_CHUNKS:
            p = jnp.dot(h, w_ref[:, c0:c1], preferred_element_type=F32)
            for b in range((c1 - c0) // LANES):
                blk = p[:, b * LANES:(b + 1) * LANES]
                gb = c0 // LANES + b
                if gb < 3:
                    cols = slice(gb * LANES, (gb + 1) * LANES)
                    qa_ref[rows, cols] = (_rope(blk, cosa, sina, 32, lane) * QSCALE).astype(BF16)
                elif gb < 6:
                    cols = slice((gb - 3) * LANES, (gb - 2) * LANES)
                    ka_ref[rows, cols] = _rope(blk, cosa, sina, 32, lane).astype(BF16)
                elif gb < 9:
                    cols = slice((gb - 6) * LANES, (gb - 5) * LANES)
                    va_ref[rows, cols] = blk.astype(BF16)
                elif gb < 15:
                    cols = slice((gb - 9) * LANES, (gb - 8) * LANES)
                    hy_ref[rows, cols] = blk.astype(BF16)
                elif gb < 18:
                    cols = slice((gb - 15) * LANES, (gb - 14) * LANES)
                    q = _rope(head_norm(blk, gq_ref[...]), cosc, sinc, 16, lane)
                    qc_ref[rows, cols] = (q * QSCALE).astype(BF16)
                elif gb == 18:
                    kc_ref[rows, :] = _rope(head_norm(blk, gk_ref[...]), cosc, sinc, 16, lane).astype(BF16)
                else:
                    vc_ref[rows, :] = blk.astype(BF16)
        return carry

    lax.fori_loop(0, IN_STEPS, sub, 0)


def _inproj(x2, g, w, tables, gq, gk, bd):
    n = x2.shape[0]
    tiles_per_seq = SEQ // IN_TM
    row = lambda width: pl.BlockSpec((IN_TM, width), lambda i: (i, 0))
    tab = pl.BlockSpec((IN_TM, LANES), lambda i: (i % tiles_per_seq, 0))
    outs = [A_WIDTH, A_WIDTH, A_WIDTH, 3 * HY_WIDTH, C_WIDTH, C_KV_WIDTH, C_KV_WIDTH]
    return pl.pallas_call(
        _inproj_kernel,
        grid=(n // IN_TM,),
        in_specs=[row(D_MODEL), _resident((1, D_MODEL)), _resident((D_MODEL, PROJ_WIDTH)),
                  tab, tab, tab, tab, _resident((1, LANES)), _resident((1, LANES)), _resident((LANES, LANES))],
        out_specs=[row(wd) for wd in outs],
        out_shape=[jax.ShapeDtypeStruct((n, wd), BF16) for wd in outs],
        scratch_shapes=[pltpu.VMEM((2, IN_RB, D_MODEL), BF16)],
        compiler_params=_params(),
        name="inproj",
    )(x2, g, w, *tables, gq, gk, bd)


DL_BQ = 128
DL_BK = 256
DL_PER_STEP = 2
DL_BLK = 256
DL_NBLK = SEQ // DL_BLK
DL_NPROB = SEQ // DL_BQ


def _regroup_matrices():
    def perm(d):
        p = np.zeros((DL_BLK, DL_BLK))
        per = DL_BLK // d
        for i in range(per):
            for r in range(d):
                p[r * per + i, d * i + r] = 1.0
        return p

    p16, p4 = perm(16), perm(4)
    return (jnp.asarray(np.concatenate([p16, p4], 0), BF16), jnp.asarray(p16.T, BF16), jnp.asarray(p4.T, BF16))


def _band_problems(probs, lane):
    lo = lax.broadcasted_iota(jnp.int32, (1, LANES), 1) < HEAD_DIM
    sels = (lo, jnp.logical_not(lo))
    npair = A_WIDTH // LANES
    scores = []
    for q, k, _, _ in probs:
        ss = []
        for j in range(npair):
            cols = slice(j * LANES, (j + 1) * LANES)
            qj, kj = q[:, cols], k[:, cols]
            for g in range(2):
                qm = jnp.where(sels[g], qj, jnp.zeros_like(qj))
                ss.append(lax.dot_general(qm, kj, (((1,), (1,)), ((), ())), preferred_element_type=F32))
        scores.append(ss)
    results = []
    for (q, _, v, mask), ss in zip(probs, scores):
        pns = []
        lse = jnp.zeros((q.shape[0], LANES), F32)
        for h, s in enumerate(ss):
            s = jnp.where(mask, s, NEG)
            m = jnp.max(s, axis=-1, keepdims=True)
            p = jnp.exp2(s - m)
            l = jnp.sum(p, axis=-1, keepdims=True)
            pns.append((p * (1.0 / l)).astype(BF16))
            lse = jnp.where(lane == h, m + jnp.log2(l), lse)
        outs = []
        for j in range(npair):
            vj = v[:, j * LANES:(j + 1) * LANES]
            acc = None
            for g in range(2):
                vm = jnp.where(sels[g], vj, jnp.zeros_like(vj))
                d = jnp.dot(pns[2 * j + g], vm, preferred_element_type=F32)
                acc = d if acc is None else acc + d
            outs.append(acc)
        results.append((jnp.concatenate(outs, axis=1), lse))
    return results


def _split_hi_lo(x):
    hi = x.astype(BF16)
    return hi, (x - hi.astype(F32)).astype(BF16)


def _dilated_kernel(q_ref, k_ref, v_ref, pf_ref, pb16_ref, pb4_ref, out_ref, xp_ref, ob_ref, l1_ref, st_ref):
    lane = lax.broadcasted_iota(jnp.int32, (DL_BQ, LANES), 1)

    def regroup(blk, carry):
        rows = pl.ds(pl.multiple_of(blk * DL_BLK, DL_BLK), DL_BLK)
        for a, src in enumerate((q_ref, k_ref, v_ref)):
            y = jnp.dot(pf_ref[...], src[rows, :], preferred_element_type=F32).astype(BF16)
            for d, base, dst in ((16, 0, a), (4, DL_BLK, 3 + a)):
                per = DL_BLK // d
                for r in range(d):
                    at = pl.ds(pl.multiple_of(r * (SEQ // d) + blk * per, per), per)
                    xp_ref[dst, at, :] = y[base + r * per:base + (r + 1) * per]
        return carry

    lax.fori_loop(0, DL_NBLK, regroup, 0)

    def banded(qr, kr, vr, seg_len, nk, write):
        row = lax.broadcasted_iota(jnp.int32, (DL_BQ, nk), 0)
        col = lax.broadcasted_iota(jnp.int32, (DL_BQ, nk), 1)

        def step(i, carry):
            probs, starts = [], []
            for u in range(DL_PER_STEP):
                t0 = pl.multiple_of((i * DL_PER_STEP + u) * DL_BQ, DL_BQ)
                seg0 = (t0 // seg_len) * seg_len
                ks = pl.multiple_of(seg0 + jnp.clip(t0 - seg0 - N_SIDE, 0, seg_len - nk), N_SIDE)
                diff = col - row + (ks - t0)
                mask = (diff <= N_SIDE) & (diff >= -N_SIDE)
                probs.append((qr[pl.ds(t0, DL_BQ), :], kr[pl.ds(ks, nk), :], vr[pl.ds(ks, nk), :], mask))
                starts.append(t0)
            for t0, (o, lse) in zip(starts, _band_problems(probs, lane)):
                write(pl.ds(t0, DL_BQ), o.astype(BF16), lse)
            return carry

        lax.fori_loop(0, DL_NPROB // DL_PER_STEP, step, 0)

    def write_natural(rows, o, lse):
        ob_ref[0, rows, :] = o
        l1_ref[rows, :] = lse

    def write_regrouped(b):
        def write(rows, o, lse):
            hi, lo = _split_hi_lo(lse)
            ob_ref[b, rows, :] = o
            st_ref[b - 1, rows, :LANES] = hi
            st_ref[b - 1, rows, LANES:] = lo
        return write

    banded(q_ref, k_ref, v_ref, SEQ, DL_BK, write_natural)
    banded(xp_ref.at[3], xp_ref.at[4], xp_ref.at[5], SEQ // 4, DL_BK, write_regrouped(1))
    banded(xp_ref.at[0], xp_ref.at[1], xp_ref.at[2], SEQ // 16, SEQ // 16, write_regrouped(2))

    lo_lane = lax.broadcasted_iota(jnp.int32, (1, LANES), 1) < HEAD_DIM

    def merge(blk, carry):
        rows = pl.ds(pl.multiple_of(blk * DL_BLK, DL_BLK), DL_BLK)

        def ungroup(b, d, pb_ref):
            per = DL_BLK // d
            chunks = []
            for r in range(d):
                at = pl.ds(pl.multiple_of(r * (SEQ // d) + blk * per, per), per)
                chunks.append(jnp.concatenate([ob_ref[b, at, :], st_ref[b - 1, at, :]], axis=1))
            nat = jnp.dot(pb_ref[...], jnp.concatenate(chunks, axis=0), preferred_element_type=F32)
            return nat[:, :A_WIDTH], nat[:, A_WIDTH:A_WIDTH + LANES] + nat[:, A_WIDTH + LANES:]

        o4, s4 = ungroup(1, 4, pb4_ref)
        o16, s16 = ungroup(2, 16, pb16_ref)
        o1, s1 = ob_ref[0, rows, :].astype(F32), l1_ref[rows, :]
        mx = jnp.maximum(jnp.maximum(s1, s4), s16)
        es = [jnp.exp2(s - mx) for s in (s1, s4, s16)]
        inv = 1.0 / (es[0] + es[1] + es[2])
        merged = []
        for j in range(A_WIDTH // LANES):
            cols = slice(j * LANES, (j + 1) * LANES)
            acc = None
            for e, o in zip(es, (o1, o4, o16)):
                w = e * inv
                wfull = jnp.where(lo_lane, jnp.broadcast_to(w[:, 2 * j:2 * j + 1], (DL_BLK, LANES)),
                                  jnp.broadcast_to(w[:, 2 * j + 1:2 * j + 2], (DL_BLK, LANES)))
                t = wfull * o[:, cols]
                acc = t if acc is None else acc + t
            merged.append(acc)
        out_ref[rows, :] = jnp.concatenate(merged, axis=1).astype(BF16)
        return carry

    lax.fori_loop(0, DL_NBLK, merge, 0)


def _dilated(qa, ka, va, batch):
    seq = pl.BlockSpec((None, SEQ, A_WIDTH), lambda b: (b, 0, 0))
    pf, pb16, pb4 = _regroup_matrices()
    shaped = lambda a: a.reshape(batch, SEQ, A_WIDTH)
    out = pl.pallas_call(
        _dilated_kernel, grid=(batch,),
        in_specs=[seq, seq, seq, _resident(pf.shape), _resident(pb16.shape), _resident(pb4.shape)],
        out_specs=seq,
        out_shape=jax.ShapeDtypeStruct((batch, SEQ, A_WIDTH), BF16),
        scratch_shapes=[pltpu.VMEM((6, SEQ, A_WIDTH), BF16), pltpu.VMEM((3, SEQ, A_WIDTH), BF16),
                        pltpu.VMEM((SEQ, LANES), F32), pltpu.VMEM((2, SEQ, 2 * LANES), BF16)],
        compiler_params=_params(), name="dilated",
    )(shaped(qa), shaped(ka), shaped(va), pf, pb16, pb4)
    return out.reshape(batch * SEQ, A_WIDTH)


GQ_TQ = 256


def _gqa_kernel(q_ref, k_ref, v_ref, o_ref, ksw_ref, vm_ref):
    lane = lax.broadcasted_iota(jnp.int32, (SEQ, LANES), 1)
    lo = lane < HEAD_DIM
    k = k_ref[...]
    v = v_ref[...]
    ksw_ref[...] = pltpu.roll(k.astype(F32), HEAD_DIM, 1).astype(BF16)
    vsw = pltpu.roll(v.astype(F32), HEAD_DIM, 1).astype(BF16)
    zero = jnp.zeros_like(v)
    vm_ref[0] = jnp.where(lo, v, zero)
    vm_ref[1] = jnp.where(lo, zero, vsw)
    vm_ref[2] = jnp.where(lo, vsw, zero)
    vm_ref[3] = jnp.where(lo, zero, v)
    lo_q = lax.broadcasted_iota(jnp.int32, (1, LANES), 1) < HEAD_DIM

    nh = 2 * (C_WIDTH // LANES)

    def blk(i, carry):
        rows = pl.ds(pl.multiple_of(i * GQ_TQ, GQ_TQ), GQ_TQ)

        def scores(h):
            j, half = divmod(h, 2)
            qj = q_ref[rows, j * LANES:(j + 1) * LANES]
            sel = lo_q if half == 0 else jnp.logical_not(lo_q)
            qm = jnp.where(sel, qj, jnp.zeros_like(qj))
            kk = k_ref[...] if h // 3 == half else ksw_ref[...]
            return lax.dot_general(qm, kk, (((1,), (1,)), ((), ())), preferred_element_type=F32)

        def softmax(s):
            p = jnp.exp2(s - jnp.max(s, axis=-1, keepdims=True))
            return p.astype(BF16), 1.0 / jnp.sum(p, axis=-1, keepdims=True)

        def values(h, pb, inv):
            return jnp.dot(pb, vm_ref[2 * (h // 3) + h % 2], preferred_element_type=F32) * inv

        pending = {0: scores(0), 1: scores(1)}
        outs = []
        for h in range(nh):
            pb, inv = softmax(pending.pop(h))
            if h + 2 < nh:
                pending[h + 2] = scores(h + 2)
            outs.append(values(h, pb, inv))
        for j in range(nh // 2):
            o_ref[rows, j * LANES:(j + 1) * LANES] = (outs[2 * j] + outs[2 * j + 1]).astype(BF16)
        return carry

    lax.fori_loop(0, SEQ // GQ_TQ, blk, 0)


def _gqa(qc, kc, vc, batch):
    spec = lambda width: pl.BlockSpec((None, SEQ, width), lambda b: (b, 0, 0))
    out = pl.pallas_call(
        _gqa_kernel, grid=(batch,),
        in_specs=[spec(C_WIDTH), spec(C_KV_WIDTH), spec(C_KV_WIDTH)],
        out_specs=spec(C_WIDTH),
        out_shape=jax.ShapeDtypeStruct((batch, SEQ, C_WIDTH), BF16),
        scratch_shapes=[pltpu.VMEM((SEQ, LANES), BF16), pltpu.VMEM((4, SEQ, LANES), BF16)],
        compiler_params=_params(), name="gqa",
    )(qc.reshape(batch, SEQ, C_WIDTH), kc.reshape(batch, SEQ, C_KV_WIDTH), vc.reshape(batch, SEQ, C_KV_WIDTH))
    return out.reshape(batch * SEQ, C_WIDTH)


def _filter_kernel(z_ref, trow_ref, w1t_ref, b1_ref, f0_ref, w2t_ref, b2_ref, f1_ref, w3t_ref, dec_ref, out_ref):
    L = SEQ
    dot = functools.partial(jnp.dot, precision=HIGHEST, preferred_element_type=F32)
    h = jnp.sin(f0_ref[...] * (dot(w1t_ref[...], z_ref[...]) + b1_ref[...]))
    h = jnp.sin(f1_ref[...] * (dot(w2t_ref[...], h) + b2_ref[...]))
    w3t = w3t_ref[...]
    dec = dec_ref[...]
    trow = trow_ref[...]
    hf = dot(w3t[:HY_WIDTH], h[:, L:]) * jnp.exp(-trow[:, L:] * dec[:HY_WIDTH])
    hb = dot(w3t[HY_WIDTH:], h[:, :L]) * jnp.exp(-trow[:, :L] * dec[HY_WIDTH:])
    col = lax.broadcasted_iota(jnp.int32, hb.shape, 1)
    hb = jnp.where(col == 0, 0.0, pltpu.roll(hb, 1, 1))
    norm = (jnp.sum(jnp.abs(hf), axis=-1, keepdims=True) + jnp.sum(jnp.abs(hb), axis=-1, keepdims=True))
    out_ref[:, :L] = hb / norm
    out_ref[:, L:] = hf / norm


def _filters(zt, trow, w1, b1, freq, w2, b2, w3, decay):
    colv = lambda a: a.reshape(-1, 1).astype(F32)
    w1t = jnp.zeros((HY_HIDDEN, LANES), F32).at[:, :HY_EMB].set(w1.T)
    order = pl.BlockSpec((2 * HY_WIDTH, HY_HIDDEN), lambda o: (o, 0))
    return pl.pallas_call(
        _filter_kernel, grid=(2,),
        in_specs=[_resident(zt.shape), _resident(trow.shape), _resident(w1t.shape), _resident((HY_HIDDEN, 1)),
                  _resident((HY_HIDDEN, 1)), _resident((HY_HIDDEN, HY_HIDDEN)), _resident((HY_HIDDEN, 1)),
                  _resident((HY_HIDDEN, 1)), order, pl.BlockSpec((2 * HY_WIDTH, 1), lambda o: (o, 0))],
        out_specs=pl.BlockSpec((HY_WIDTH, 2 * SEQ), lambda o: (o, 0)),
        out_shape=jax.ShapeDtypeStruct((2 * HY_WIDTH, 2 * SEQ), F32),
        compiler_params=_params(), name="filters",
    )(zt, trow, w1t, colv(b1), colv(freq[0]), w2.T, colv(b2), colv(freq[1]), w3.T, colv(decay))


HY_TB = 256
HY_NB = SEQ // HY_TB


def _hyena_kernel(cw_ref, cb_ref, d_ref, pv_ref, px1_ref, px2_ref, g0_ref, g1_ref, o_ref,
                  gsh_ref, ust_ref, acc_ref):
    c = pl.program_id(0)
    nb = pv_ref.shape[0]
    col = lax.broadcasted_iota(jnp.int32, (nb, SEQ), 1)

    def dwconv(p_ref, ch):
        p = p_ref[...].astype(F32)
        prev = jnp.where(col == 0, 0.0, pltpu.roll(p, 1, 1))
        nxt = jnp.where(col == SEQ - 1, 0.0, pltpu.roll(p, SEQ - 1, 1))
        return prev * cw_ref[0, ch] + p * cw_ref[1, ch] + nxt * cw_ref[2, ch] + cb_ref[ch]

    def longconv(u, g_ref, skip):
        g = jnp.broadcast_to(g_ref[...], (LANES, 2 * SEQ))
        gsh_ref[...] = pltpu.roll(g, 0, 1, stride=1, stride_axis=0).astype(BF16)
        ub = u.astype(BF16)
        for tb in range(HY_NB):
            ust_ref[tb * nb:(tb + 1) * nb, :] = ub[:, tb * HY_TB:(tb + 1) * HY_TB]
        acc_ref[...] = jnp.zeros_like(acc_ref)
        for dl in range(-(HY_NB - 1), HY_NB):
            x0 = HY_TB * (dl + HY_NB)
            rhs = jnp.concatenate([gsh_ref[:, x0:x0 + HY_TB], gsh_ref[:, x0 - LANES:x0 - LANES + HY_TB]], axis=0)
            b0, b1 = max(0, -dl), min(HY_NB, HY_NB - dl)
            acc_ref[(b0 + dl) * nb:(b1 + dl) * nb, :] += jnp.dot(
                ust_ref[b0 * nb:b1 * nb, :], rhs, preferred_element_type=F32)
        y = jnp.concatenate([acc_ref[tb * nb:(tb + 1) * nb, :] for tb in range(HY_NB)], axis=1)
        return y + skip * u

    v = dwconv(pv_ref, c)
    x1 = dwconv(px1_ref, HY_WIDTH + c)
    x2 = dwconv(px2_ref, 2 * HY_WIDTH + c)
    z = x1 * longconv(v, g0_ref, d_ref[0, c])
    o_ref[...] = (x2 * longconv(z, g1_ref, d_ref[1, c])).astype(BF16)


def _hyena(pt, gt, conv_w, conv_b, dbias):
    nb = pt.shape[1]
    smem = pl.BlockSpec(memory_space=pltpu.SMEM)
    chan = lambda off: pl.BlockSpec((None, nb, SEQ), lambda c: (c + off, 0, 0))
    filt = lambda off: pl.BlockSpec((None, 1, 2 * SEQ), lambda c: (c + off, 0, 0))
    gt3 = gt.reshape(2 * HY_WIDTH, 1, 2 * SEQ)
    return pl.pallas_call(
        _hyena_kernel, grid=(HY_WIDTH,),
        in_specs=[smem, smem, smem, chan(0), chan(HY_WIDTH), chan(2 * HY_WIDTH), filt(0), filt(HY_WIDTH)],
        out_specs=pl.BlockSpec((None, nb, SEQ), lambda c: (c, 0, 0)),
        out_shape=jax.ShapeDtypeStruct((HY_WIDTH, nb, SEQ), BF16),
        scratch_shapes=[pltpu.VMEM((LANES, 2 * SEQ), BF16), pltpu.VMEM((HY_NB * nb, HY_TB), BF16),
                        pltpu.VMEM((HY_NB * nb, HY_TB), F32)],
        compiler_params=_params(), name="hyena",
    )(conv_w, conv_b, dbias, pt, pt, pt, gt3, gt3)


OP_TM = 1024
OP_RB = 256


def _outproj_kernel(oa_ref, hy_ref, oc_ref, w_ref, x_ref, g_ref, out_ref):
    def sub(i, carry):
        rows = pl.ds(pl.multiple_of(i * OP_RB, OP_RB), OP_RB)
        mix = jnp.dot(oa_ref[rows, :], w_ref[:A_WIDTH, :], preferred_element_type=F32)
        mix += jnp.dot(hy_ref[rows, :], w_ref[A_WIDTH:A_WIDTH + HY_WIDTH, :], preferred_element_type=F32)
        mix += jnp.dot(oc_ref[rows, :], w_ref[A_WIDTH + HY_WIDTH:, :], preferred_element_type=F32)
        out_ref[rows, :] = x_ref[rows, :] + _rms(mix, g_ref[...])
        return carry

    lax.fori_loop(0, OP_TM // OP_RB, sub, 0)


def _outproj(oa, hyo, oc, w, x2, g):
    n = x2.shape[0]
    row = lambda width: pl.BlockSpec((OP_TM, width), lambda i: (i, 0))
    return pl.pallas_call(
        _outproj_kernel, grid=(n // OP_TM,),
        in_specs=[row(A_WIDTH), row(HY_WIDTH), row(C_WIDTH),
                  _resident((D_MODEL, D_MODEL)), row(D_MODEL), _resident((1, D_MODEL))],
        out_specs=row(D_MODEL),
        out_shape=jax.ShapeDtypeStruct((n, D_MODEL), F32),
        compiler_params=_params(), name="outproj",
    )(oa, hyo, oc, w, x2, g)


FF_TM = 512
FF_RB = 256
FF_HALO = 16


def _ffn_kernel(x_ref, xp_ref, xn_ref, gpre_ref, wg_ref, wu_ref, cw_ref, cb_ref, wd_ref, gpost_ref, out_ref, xs_ref):
    i = pl.program_id(0)
    tiles_per_seq = SEQ // FF_TM
    first = (i % tiles_per_seq) == 0
    last = (i % tiles_per_seq) == tiles_per_seq - 1
    xs_ref[:FF_HALO, :] = jnp.where(first, 0.0, xp_ref[...])
    xs_ref[FF_HALO:FF_HALO + FF_TM, :] = x_ref[...]
    xs_ref[FF_HALO + FF_TM:, :] = jnp.where(last, 0.0, xn_ref[...])

    def sub(s, carry):
        r0 = pl.multiple_of(s * FF_RB, FF_RB)
        xh = xs_ref[pl.ds(r0, FF_RB + 2 * FF_HALO), :]
        h = _rms(xh, gpre_ref[...]).astype(BF16)
        gate = jnp.dot(h, wg_ref[...], preferred_element_type=F32)
        up = jnp.dot(h[FF_HALO:FF_HALO + FF_RB], wu_ref[...], preferred_element_type=F32)
        gc = (gate[FF_HALO - 1:FF_HALO - 1 + FF_RB] * cw_ref[0:1, :]
              + gate[FF_HALO:FF_HALO + FF_RB] * cw_ref[1:2, :]
              + gate[FF_HALO + 1:FF_HALO + 1 + FF_RB] * cw_ref[2:3, :] + cb_ref[...])
        act = (jax.nn.gelu(gc, approximate=True) * up).astype(BF16)
        f = jnp.dot(act, wd_ref[...], preferred_element_type=F32)
        x = xh[FF_HALO:FF_HALO + FF_RB]
        out_ref[pl.ds(r0, FF_RB), :] = x + _rms(f, gpost_ref[...])
        return carry

    lax.fori_loop(0, FF_TM // FF_RB, sub, 0)


def _ffn(x2, gpre, wg, wu, cw, cb, wd, gpost):
    n = x2.shape[0]
    per = FF_TM // FF_HALO
    last_blk = n // FF_HALO - 1
    return pl.pallas_call(
        _ffn_kernel, grid=(n // FF_TM,),
        in_specs=[pl.BlockSpec((FF_TM, D_MODEL), lambda i: (i, 0)),
                  pl.BlockSpec((FF_HALO, D_MODEL), lambda i: (jnp.maximum(i * per - 1, 0), 0)),
                  pl.BlockSpec((FF_HALO, D_MODEL), lambda i: (jnp.minimum((i + 1) * per, last_blk), 0)),
                  _resident((1, D_MODEL)), _resident((D_MODEL, D_FF)), _resident((D_MODEL, D_FF)),
                  _resident((3, D_FF)), _resident((1, D_FF)), _resident((D_FF, D_MODEL)), _resident((1, D_MODEL))],
        out_specs=pl.BlockSpec((FF_TM, D_MODEL), lambda i: (i, 0)),
        out_shape=jax.ShapeDtypeStruct((n, D_MODEL), F32),
        scratch_shapes=[pltpu.VMEM((FF_TM + 2 * FF_HALO, D_MODEL), F32)],
        compiler_params=_params(), name="ffn",
    )(x2, x2, x2, gpre, wg, wu, cw, cb, wd, gpost)


def kernel(x, g_mix_pre, g_mix_post, g_ffn_pre, g_ffn_post, w_in, w_out, g_q, g_k, hy_conv_w, hy_conv_b, hy_w1, hy_b1, hy_freq, hy_w2, hy_b2, hy_w3, hy_decay, hy_d, ffn_w_gate, ffn_w_up, ffn_conv_w, ffn_conv_b, ffn_w_down):
    batch = x.shape[0]
    n = batch * SEQ
    tables = _rope_tables()
    bd = _head_mean_matrix()
    zt, trow = _hyena_positions()
    rowv = lambda a: a.reshape(1, -1).astype(F32)
    x2 = x.reshape(n, D_MODEL)
    for i in range(DEPTH):
        qa, ka, va, hy, qc, kc, vc = _inproj(
            x2, rowv(g_mix_pre[i]), w_in[i].astype(BF16), tables,
            rowv(jnp.tile(g_q[i], 2)), rowv(jnp.tile(g_k[i], 2)), bd)
        oa = _dilated(qa, ka, va, batch)
        oc = _gqa(qc, kc, vc, batch)
        gt = _filters(zt, trow, hy_w1[i], hy_b1[i], hy_freq[i], hy_w2[i], hy_b2[i], hy_w3[i], hy_decay[i])
        pt = jnp.transpose(hy.reshape(batch, SEQ, 3 * HY_WIDTH), (2, 0, 1))
        ot = _hyena(pt, gt, hy_conv_w[i].astype(F32), hy_conv_b[i].astype(F32), hy_d[i].astype(F32))
        hyo = jnp.transpose(ot, (1, 2, 0)).reshape(n, HY_WIDTH)
        x2 = _outproj(oa, hyo, oc, w_out[i].astype(BF16), x2, rowv(g_mix_post[i]))
        x2 = _ffn(x2, rowv(g_ffn_pre[i]), ffn_w_gate[i].astype(BF16), ffn_w_up[i].astype(BF16),
                  ffn_conv_w[i].astype(F32), rowv(ffn_conv_b[i]), ffn_w_down[i].astype(BF16),
                  rowv(g_ffn_post[i]))
    return x2.reshape(batch, SEQ, D_MODEL)
```

```python
import functools
import math

import numpy as np
import jax
import jax.numpy as jnp
from jax import lax
from jax.experimental import pallas as pl
from jax.experimental.pallas import tpu as pltpu

F32 = jnp.float32
BF16 = jnp.bfloat16

D_MODEL = 1024
SEQ = 2048
DEPTH = 2
HEAD_DIM = 64
A_WIDTH = 384
HY_WIDTH = 256
C_WIDTH = 384
C_KV_WIDTH = 128
PROJ_WIDTH = 2560
GRID_W = 64
ROPE_THETA = 10000.0
HY_BANDS = 16
HY_EMB = 33
HY_HIDDEN = 64
D_FF = 2816
EPS = 1e-6
N_SIDE = 64
QSCALE = HEAD_DIM ** -0.5 * math.log2(math.e)
NEG = -1e30

LANES = 128
VMEM_LIMIT = 56 * 1024 * 1024

HIGHEST = lax.Precision.HIGHEST


def _params(n_grid_dims=1):
    return pltpu.CompilerParams(
        dimension_semantics=("arbitrary",) * n_grid_dims, vmem_limit_bytes=VMEM_LIMIT)


def _resident(shape):
    nd = len(shape)
    return pl.BlockSpec(shape, lambda *_: (0,) * nd, pipeline_mode=pl.Buffered(1))


def _rope_tables():
    def angles(pos, dim):
        freqs = ROPE_THETA ** (-np.arange(0, dim, 2, dtype=np.float64) / dim)
        ang = pos.astype(np.float64)[:, None] * freqs[None, :]
        return np.cos(ang), np.sin(ang)

    pos = np.arange(SEQ)
    c, s = angles(pos, HEAD_DIM)
    cos_a = np.tile(np.concatenate([c, c], -1), (1, 2))
    sin_a = np.tile(np.concatenate([-s, s], -1), (1, 2))
    cr, sr = angles(pos // GRID_W, HEAD_DIM // 2)
    cc, sc = angles(pos % GRID_W, HEAD_DIM // 2)
    cos_c = np.tile(np.concatenate([cr, cr, cc, cc], -1), (1, 2))
    sin_c = np.tile(np.concatenate([-sr, sr, -sc, sc], -1), (1, 2))
    return tuple(jnp.asarray(t, F32) for t in (cos_a, sin_a, cos_c, sin_c))


def _head_mean_matrix():
    m = np.kron(np.eye(LANES // HEAD_DIM), np.full((HEAD_DIM, HEAD_DIM), 1.0 / HEAD_DIM))
    return jnp.asarray(np.concatenate([m, m], 0), BF16)


def _hyena_positions():
    L = SEQ
    t = np.linspace(0.0, 1.0, L)
    bands = np.linspace(1e-4, HY_BANDS - 1, HY_BANDS)
    ang = 2.0 * math.pi * bands[None, :] * np.arange(L)[:, None] / L
    z = np.concatenate([t[:, None], np.cos(ang), -np.sin(ang)], -1)
    zt = np.zeros((LANES, 2 * L))
    zt[:HY_EMB, :L] = z[::-1].T
    zt[:HY_EMB, L:] = z.T
    trow = np.concatenate([t[::-1], t])[None, :]
    return jnp.asarray(zt, F32), jnp.asarray(trow, F32)


def _rope(x, cos, sin_signed, half, lane):
    first = (lane & (2 * half - 1)) < half
    swapped = jnp.where(first, pltpu.roll(x, LANES - half, 1), pltpu.roll(x, half, 1))
    return x * cos + swapped * sin_signed


def _rms(x, gain):
    ms = jnp.mean(x * x, axis=-1, keepdims=True)
    return x * lax.rsqrt(ms + EPS) * gain


IN_TM = 1024
IN_RB = 512
IN_STEPS = IN_TM // IN_RB
_IN_CHUNKS = ((0, PROJ_WIDTH),)


def _inproj_kernel(x_ref, g_ref, w_ref, cosa_ref, sina_ref, cosc_ref, sinc_ref, gq_ref, gk_ref, bd_ref,
                   qa_ref, ka_ref, va_ref, hy_ref, qc_ref, kc_ref, vc_ref, h_ref):
    lane = lax.broadcasted_iota(jnp.int32, (IN_RB, LANES), 1)

    def normed(i):
        rows = pl.ds(pl.multiple_of(i * IN_RB, IN_RB), IN_RB)
        return _rms(x_ref[rows, :], g_ref[...]).astype(BF16)

    h_ref[0] = normed(0)

    def head_norm(v, gain):
        hi_lo = jnp.concatenate(_split_hi_lo(v * v), axis=1)
        ms = jnp.dot(hi_lo, bd_ref[...], preferred_element_type=F32)
        return v * lax.rsqrt(ms + EPS) * gain

    def sub(i, carry):
        rows = pl.ds(pl.multiple_of(i * IN_RB, IN_RB), IN_RB)
        slot = i % 2
        h = h_ref[slot]
        h_ref[1 - slot] = normed(jnp.minimum(i + 1, IN_STEPS - 1))
        cosa, sina = cosa_ref[rows, :], sina_ref[rows, :]
        cosc, sinc = cosc_ref[rows, :], sinc_ref[rows, :]
        for c0, c1 in _IN_CHUNKS:
            p = jnp.dot(h, w_ref[:, c0:c1], preferred_element_type=F32)
            for b in range((c1 - c0) // LANES):
                blk = p[:, b * LANES:(b + 1) * LANES]
                gb = c0 // LANES + b
                if gb < 3:
                    cols = slice(gb * LANES, (gb + 1) * LANES)
                    qa_ref[rows, cols] = (_rope(blk, cosa, sina, 32, lane) * QSCALE).astype(BF16)
                elif gb < 6:
                    cols = slice((gb - 3) * LANES, (gb - 2) * LANES)
                    ka_ref[rows, cols] = _rope(blk, cosa, sina, 32, lane).astype(BF16)
                elif gb < 9:
                    cols = slice((gb - 6) * LANES, (gb - 5) * LANES)
                    va_ref[rows, cols] = blk.astype(BF16)
                elif gb < 15:
                    cols = slice((gb - 9) * LANES, (gb - 8) * LANES)
                    hy_ref[rows, cols] = blk.astype(BF16)
                elif gb < 18:
                    cols = slice((gb - 15) * LANES, (gb - 14) * LANES)
                    q = _rope(head_norm(blk, gq_ref[...]), cosc, sinc, 16, lane)
                    qc_ref[rows, cols] = (q * QSCALE).astype(BF16)
                elif gb == 18:
                    kc_ref[rows, :] = _rope(head_norm(blk, gk_ref[...]), cosc, sinc, 16, lane).astype(BF16)
                else:
                    vc_ref[rows, :] = blk.astype(BF16)
        return carry

    lax.fori_loop(0, IN_STEPS, sub, 0)


def _inproj(x2, g, w, tables, gq, gk, bd):
    n = x2.shape[0]
    tiles_per_seq = SEQ // IN_TM
    row = lambda width: pl.BlockSpec((IN_TM, width), lambda i: (i, 0))
    tab = pl.BlockSpec((IN_TM, LANES), lambda i: (i % tiles_per_seq, 0))
    outs = [A_WIDTH, A_WIDTH, A_WIDTH, 3 * HY_WIDTH, C_WIDTH, C_KV_WIDTH, C_KV_WIDTH]
    return pl.pallas_call(
        _inproj_kernel,
        grid=(n // IN_TM,),
        in_specs=[row(D_MODEL), _resident((1, D_MODEL)), _resident((D_MODEL, PROJ_WIDTH)),
                  tab, tab, tab, tab, _resident((1, LANES)), _resident((1, LANES)), _resident((2 * LANES, LANES))],
        out_specs=[row(wd) for wd in outs],
        out_shape=[jax.ShapeDtypeStruct((n, wd), BF16) for wd in outs],
        scratch_shapes=[pltpu.VMEM((2, IN_RB, D_MODEL), BF16)],
        compiler_params=_params(),
        name="inproj",
    )(x2, g, w, *tables, gq, gk, bd)


DL_BQ = 128
DL_BK = 256
DL_PER_STEP = 2
DL_BLK = 256
DL_NBLK = SEQ // DL_BLK
DL_NPROB = SEQ // DL_BQ


DL_R = 16
DL_SEG = SEQ // DL_R


def _stat_lane(h):
    return HEAD_DIM * (h % 2) + h // 2


def _dilated_constants():
    per = DL_BLK // DL_R
    p = np.zeros((DL_BLK, DL_BLK))
    for i in range(per):
        for r in range(DL_R):
            p[r * per + i, DL_R * i + r] = 1.0
    e = np.zeros((LANES, A_WIDTH))
    for h in range(A_WIDTH // HEAD_DIM):
        e[_stat_lane(h), h * HEAD_DIM:(h + 1) * HEAD_DIM] = 1.0
    return jnp.asarray(p, BF16), jnp.asarray(p.T, BF16), jnp.asarray(np.concatenate([e, e], 0), BF16)


def _band_problems(probs, lane):
    lo = lax.broadcasted_iota(jnp.int32, (1, LANES), 1) < HEAD_DIM
    npair = A_WIDTH // LANES
    scores = []
    for q, k, _, _ in probs:
        nq = q.shape[0]
        ss = []
        for j in range(npair):
            cols = slice(j * LANES, (j + 1) * LANES)
            qj, zero = q[:, cols], jnp.zeros_like(q[:, cols])
            qst = jnp.concatenate([jnp.where(lo, qj, zero), jnp.where(lo, zero, qj)], axis=0)
            s2 = lax.dot_general(qst, k[:, cols], (((1,), (1,)), ((), ())), preferred_element_type=F32)
            ss += [s2[:nq], s2[nq:]]
        scores.append(ss)
    results = []
    for (q, _, v, mask), ss in zip(probs, scores):
        nq, nk = q.shape[0], v.shape[0]
        ones = jnp.ones((nk, LANES), BF16)
        ps = []
        mtile = jnp.zeros((nq, LANES), F32)
        for h, s in enumerate(ss):
            s = jnp.where(mask, s, NEG)
            m = jnp.max(s, axis=-1, keepdims=True)
            ps.append(jnp.exp2(s - m).astype(BF16))
            mtile = jnp.where(lane == _stat_lane(h), m, mtile)
        outs = []
        ltile = jnp.ones((nq, LANES), F32)
        for j in range(npair):
            vext = jnp.concatenate([v[:, j * LANES:(j + 1) * LANES], ones], axis=1)
            d = jnp.dot(jnp.concatenate(ps[2 * j:2 * j + 2], axis=0), vext, preferred_element_type=F32)
            num0, den0, num1, den1 = d[:nq, :LANES], d[:nq, LANES:], d[nq:, :LANES], d[nq:, LANES:]
            outs.append(jnp.where(lo, num0 * (1.0 / den0), num1 * (1.0 / den1)))
            ltile = jnp.where(lane == j, den0, jnp.where(lane == HEAD_DIM + j, den1, ltile))
        results.append((jnp.concatenate(outs, axis=1), mtile + jnp.log2(ltile)))
    return results


def _split_hi_lo(x):
    hi = x.astype(BF16)
    return hi, (x - hi.astype(F32)).astype(BF16)


def _dilated_kernel(q_ref, k_ref, v_ref, pf_ref, pb_ref, ex_ref, out_ref, xp_ref, ob_ref, l1_ref, st_ref):
    lane = lax.broadcasted_iota(jnp.int32, (DL_BQ, LANES), 1)
    per = DL_BLK // DL_R

    def block_chunks(blk):
        return [pl.ds(pl.multiple_of(r * DL_SEG + blk * per, per), per) for r in range(DL_R)]

    def regroup(blk, carry):
        rows = pl.ds(pl.multiple_of(blk * DL_BLK, DL_BLK), DL_BLK)
        x = jnp.concatenate([q_ref[rows, :], k_ref[rows, :], v_ref[rows, :]], axis=1)
        y = jnp.dot(pf_ref[...], x, preferred_element_type=F32).astype(BF16)
        for r, at in enumerate(block_chunks(blk)):
            for a in range(3):
                xp_ref[a, at, :] = y[r * per:(r + 1) * per, a * A_WIDTH:(a + 1) * A_WIDTH]
        return carry

    lax.fori_loop(0, DL_NBLK, regroup, 0)

    def run(n_steps, gather, scatter):
        def step(i, carry):
            ids = [i * DL_PER_STEP + u for u in range(DL_PER_STEP)]
            for p, (o, lse) in zip(ids, _band_problems([gather(p) for p in ids], lane)):
                scatter(p, o.astype(BF16), lse)
            return carry

        lax.fori_loop(0, n_steps, step, 0)

    def band_mask(diff):
        return (diff <= N_SIDE) & (diff >= -N_SIDE)

    def put_regrouped(b, rows, o, lse):
        hi, lo = _split_hi_lo(lse)
        ob_ref[b, rows, :] = o
        st_ref[b - 1, rows, :LANES] = hi
        st_ref[b - 1, rows, LANES:] = lo

    row = lax.broadcasted_iota(jnp.int32, (DL_BQ, DL_BK), 0)
    col = lax.broadcasted_iota(jnp.int32, (DL_BQ, DL_BK), 1)

    def gather1(p):
        t0 = pl.multiple_of(p * DL_BQ, DL_BQ)
        ks = pl.multiple_of(jnp.clip(t0 - N_SIDE, 0, SEQ - DL_BK), N_SIDE)
        keys = pl.ds(ks, DL_BK)
        return q_ref[pl.ds(t0, DL_BQ), :], k_ref[keys, :], v_ref[keys, :], band_mask(col - row + (ks - t0))

    def scatter1(p, o, lse):
        rows = pl.ds(pl.multiple_of(p * DL_BQ, DL_BQ), DL_BQ)
        ob_ref[0, rows, :] = o
        l1_ref[rows, :] = lse

    run(DL_NPROB // DL_PER_STEP, gather1, scatter1)

    n4, qrows, krows = 4, DL_BQ // 4, DL_BK // 4
    slab_of = lambda idx, n: lax.shift_right_logical(idx, n.bit_length() - 1)
    ddiff = 4 * ((col & (krows - 1)) - (row & (qrows - 1))) + (slab_of(col, krows) - slab_of(row, qrows))

    def gather4(p):
        c, b = p // n4, p % n4
        q0 = pl.multiple_of(b * qrows, qrows)
        ks = pl.multiple_of(jnp.clip(q0 - N_SIDE // 4, 0, DL_SEG - krows), N_SIDE // 4)
        slabs = [(c + 4 * s) * DL_SEG for s in range(4)]
        take = lambda a, start, n: jnp.concatenate([xp_ref[a, pl.ds(s0 + start, n), :] for s0 in slabs], axis=0)
        return (take(0, q0, qrows), take(1, ks, krows), take(2, ks, krows), band_mask(ddiff + 4 * (ks - q0)))

    def scatter4(p, o, lse):
        c, b = p // n4, p % n4
        for s in range(4):
            rows = pl.ds(pl.multiple_of((c + 4 * s) * DL_SEG + b * qrows, qrows), qrows)
            put_regrouped(1, rows, o[s * qrows:(s + 1) * qrows], lse[s * qrows:(s + 1) * qrows])

    run(DL_NPROB // DL_PER_STEP, gather4, scatter4)

    mask16 = band_mask(lax.broadcasted_iota(jnp.int32, (DL_BQ, DL_SEG), 1)
                       - lax.broadcasted_iota(jnp.int32, (DL_BQ, DL_SEG), 0))

    def gather16(p):
        rows = pl.ds(pl.multiple_of(p * DL_SEG, DL_SEG), DL_SEG)
        return xp_ref[0, rows, :], xp_ref[1, rows, :], xp_ref[2, rows, :], mask16

    def scatter16(p, o, lse):
        put_regrouped(2, pl.ds(pl.multiple_of(p * DL_SEG, DL_SEG), DL_SEG), o, lse)

    run(DL_R // DL_PER_STEP, gather16, scatter16)

    def merge(blk, carry):
        rows = pl.ds(pl.multiple_of(blk * DL_BLK, DL_BLK), DL_BLK)
        z = jnp.concatenate(
            [jnp.concatenate([ob_ref[1, at, :], st_ref[0, at, :], ob_ref[2, at, :], st_ref[1, at, :]], axis=1)
             for at in block_chunks(blk)], axis=0)
        nat = jnp.dot(pb_ref[...], z, preferred_element_type=F32)
        w0 = A_WIDTH + 2 * LANES
        o4, s4 = nat[:, :A_WIDTH], nat[:, A_WIDTH:A_WIDTH + LANES] + nat[:, A_WIDTH + LANES:w0]
        o16, s16 = nat[:, w0:w0 + A_WIDTH], nat[:, w0 + A_WIDTH:w0 + A_WIDTH + LANES] + nat[:, w0 + A_WIDTH + LANES:]
        o1, s1 = ob_ref[0, rows, :].astype(F32), l1_ref[rows, :]
        mx = jnp.maximum(jnp.maximum(s1, s4), s16)
        es = [jnp.exp2(s - mx) for s in (s1, s4, s16)]
        inv = 1.0 / (es[0] + es[1] + es[2])
        acc = None
        for e, o in zip(es, (o1, o4, o16)):
            wfull = jnp.dot(jnp.concatenate(_split_hi_lo(e * inv), axis=1), ex_ref[...], preferred_element_type=F32)
            acc = wfull * o if acc is None else acc + wfull * o
        out_ref[rows, :] = acc.astype(BF16)
        return carry

    lax.fori_loop(0, DL_NBLK, merge, 0)


def _dilated(qa, ka, va, batch):
    seq = pl.BlockSpec((None, SEQ, A_WIDTH), lambda b: (b, 0, 0))
    pf, pb, ex = _dilated_constants()
    shaped = lambda a: a.reshape(batch, SEQ, A_WIDTH)
    out = pl.pallas_call(
        _dilated_kernel, grid=(batch,),
        in_specs=[seq, seq, seq, _resident(pf.shape), _resident(pb.shape), _resident(ex.shape)],
        out_specs=seq,
        out_shape=jax.ShapeDtypeStruct((batch, SEQ, A_WIDTH), BF16),
        scratch_shapes=[pltpu.VMEM((3, SEQ, A_WIDTH), BF16), pltpu.VMEM((3, SEQ, A_WIDTH), BF16),
                        pltpu.VMEM((SEQ, LANES), F32), pltpu.VMEM((2, SEQ, 2 * LANES), BF16)],
        compiler_params=_params(), name="dilated",
    )(shaped(qa), shaped(ka), shaped(va), pf, pb, ex)
    return out.reshape(batch * SEQ, A_WIDTH)


GQ_TQ = 256


def _gqa_kernel(q_ref, k_ref, v_ref, o_ref, ksw_ref, vm_ref):
    lane = lax.broadcasted_iota(jnp.int32, (SEQ, LANES), 1)
    lo = lane < HEAD_DIM
    k = k_ref[...]
    v = v_ref[...]
    ksw_ref[...] = pltpu.roll(k.astype(F32), HEAD_DIM, 1).astype(BF16)
    vsw = pltpu.roll(v.astype(F32), HEAD_DIM, 1).astype(BF16)
    zero = jnp.zeros_like(v)
    vm_ref[0, :, :LANES] = jnp.where(lo, v, zero)
    vm_ref[1, :, :LANES] = jnp.where(lo, zero, vsw)
    vm_ref[2, :, :LANES] = jnp.where(lo, vsw, zero)
    vm_ref[3, :, :LANES] = jnp.where(lo, zero, v)
    for idx in range(4):
        vm_ref[idx, :, LANES:] = jnp.ones((SEQ, LANES), BF16)
    lo_q = lax.broadcasted_iota(jnp.int32, (1, LANES), 1) < HEAD_DIM

    nh = 2 * (C_WIDTH // LANES)

    def blk(i, carry):
        rows = pl.ds(pl.multiple_of(i * GQ_TQ, GQ_TQ), GQ_TQ)

        def scores(h):
            j, half = divmod(h, 2)
            qj = q_ref[rows, j * LANES:(j + 1) * LANES]
            sel = lo_q if half == 0 else jnp.logical_not(lo_q)
            qm = jnp.where(sel, qj, jnp.zeros_like(qj))
            kk = k_ref[...] if h // 3 == half else ksw_ref[...]
            return lax.dot_general(qm, kk, (((1,), (1,)), ((), ())), preferred_element_type=F32)

        def softmax(s):
            return jnp.exp2(s - jnp.max(s, axis=-1, keepdims=True)).astype(BF16)

        def values(h, pb):
            d = jnp.dot(pb, vm_ref[2 * (h // 3) + h % 2], preferred_element_type=F32)
            return d[:, :LANES] * (1.0 / d[:, LANES:])

        pending = {0: scores(0), 1: scores(1)}
        outs = []
        for h in range(nh):
            pb = softmax(pending.pop(h))
            if h + 2 < nh:
                pending[h + 2] = scores(h + 2)
            outs.append(values(h, pb))
        for j in range(nh // 2):
            o_ref[rows, j * LANES:(j + 1) * LANES] = (outs[2 * j] + outs[2 * j + 1]).astype(BF16)
        return carry

    lax.fori_loop(0, SEQ // GQ_TQ, blk, 0)


def _gqa(qc, kc, vc, batch):
    spec = lambda width: pl.BlockSpec((None, SEQ, width), lambda b: (b, 0, 0))
    out = pl.pallas_call(
        _gqa_kernel, grid=(batch,),
        in_specs=[spec(C_WIDTH), spec(C_KV_WIDTH), spec(C_KV_WIDTH)],
        out_specs=spec(C_WIDTH),
        out_shape=jax.ShapeDtypeStruct((batch, SEQ, C_WIDTH), BF16),
        scratch_shapes=[pltpu.VMEM((SEQ, LANES), BF16), pltpu.VMEM((4, SEQ, 2 * LANES), BF16)],
        compiler_params=_params(), name="gqa",
    )(qc.reshape(batch, SEQ, C_WIDTH), kc.reshape(batch, SEQ, C_KV_WIDTH), vc.reshape(batch, SEQ, C_KV_WIDTH))
    return out.reshape(batch * SEQ, C_WIDTH)


def _filter_kernel(z_ref, trow_ref, w1t_ref, b1_ref, f0_ref, w2t_ref, b2_ref, f1_ref, w3t_ref, dec_ref, out_ref):
    L = SEQ
    dot = functools.partial(jnp.dot, precision=HIGHEST, preferred_element_type=F32)
    h = jnp.sin(f0_ref[...] * (dot(w1t_ref[...], z_ref[...]) + b1_ref[...]))
    h = jnp.sin(f1_ref[...] * (dot(w2t_ref[...], h) + b2_ref[...]))
    w3t = w3t_ref[...]
    dec = dec_ref[...]
    trow = trow_ref[...]
    hf = dot(w3t[:HY_WIDTH], h[:, L:]) * jnp.exp(-trow[:, L:] * dec[:HY_WIDTH])
    hb = dot(w3t[HY_WIDTH:], h[:, :L]) * jnp.exp(-trow[:, :L] * dec[HY_WIDTH:])
    col = lax.broadcasted_iota(jnp.int32, hb.shape, 1)
    hb = jnp.where(col == 0, 0.0, pltpu.roll(hb, 1, 1))
    norm = (jnp.sum(jnp.abs(hf), axis=-1, keepdims=True) + jnp.sum(jnp.abs(hb), axis=-1, keepdims=True))
    out_ref[:, :L] = hb / norm
    out_ref[:, L:] = hf / norm


def _filters(zt, trow, w1, b1, freq, w2, b2, w3, decay):
    colv = lambda a: a.reshape(-1, 1).astype(F32)
    w1t = jnp.zeros((HY_HIDDEN, LANES), F32).at[:, :HY_EMB].set(w1.T)
    order = pl.BlockSpec((2 * HY_WIDTH, HY_HIDDEN), lambda o: (o, 0))
    return pl.pallas_call(
        _filter_kernel, grid=(2,),
        in_specs=[_resident(zt.shape), _resident(trow.shape), _resident(w1t.shape), _resident((HY_HIDDEN, 1)),
                  _resident((HY_HIDDEN, 1)), _resident((HY_HIDDEN, HY_HIDDEN)), _resident((HY_HIDDEN, 1)),
                  _resident((HY_HIDDEN, 1)), order, pl.BlockSpec((2 * HY_WIDTH, 1), lambda o: (o, 0))],
        out_specs=pl.BlockSpec((HY_WIDTH, 2 * SEQ), lambda o: (o, 0)),
        out_shape=jax.ShapeDtypeStruct((2 * HY_WIDTH, 2 * SEQ), F32),
        compiler_params=_params(), name="filters",
    )(zt, trow, w1t, colv(b1), colv(freq[0]), w2.T, colv(b2), colv(freq[1]), w3.T, colv(decay))


HY_TB = 256
HY_NB = SEQ // HY_TB


HY_CH = 2


def _hyena_kernel(cw_ref, cb_ref, d_ref, pv_ref, px1_ref, px2_ref, g0_ref, g1_ref, o_ref,
                  gsh_ref, ust_ref, acc_ref):
    c0 = pl.program_id(0) * HY_CH
    nb = pv_ref.shape[1]
    col = lax.broadcasted_iota(jnp.int32, (nb, SEQ), 1)
    chans = range(HY_CH)

    for cc in chans:
        for order, g_ref in enumerate((g0_ref, g1_ref)):
            g = jnp.broadcast_to(g_ref[cc], (LANES, 2 * SEQ))
            gsh_ref[2 * cc + order] = pltpu.roll(g, 0, 1, stride=1, stride_axis=0).astype(BF16)

    def dwconv(p_ref, cc, ch):
        p = p_ref[cc].astype(F32)
        prev = jnp.where(col == 0, 0.0, pltpu.roll(p, 1, 1))
        nxt = jnp.where(col == SEQ - 1, 0.0, pltpu.roll(p, SEQ - 1, 1))
        return prev * cw_ref[0, ch] + p * cw_ref[1, ch] + nxt * cw_ref[2, ch] + cb_ref[ch]

    def longconv(cc, order, u):
        gsh = gsh_ref.at[2 * cc + order]
        ust, acc = ust_ref.at[cc], acc_ref.at[cc]
        ub = u.astype(BF16)
        for tb in range(HY_NB):
            ust[tb * nb:(tb + 1) * nb, :] = ub[:, tb * HY_TB:(tb + 1) * HY_TB]
        acc[...] = jnp.zeros(acc.shape, F32)
        for dl in range(-(HY_NB - 1), HY_NB):
            x0 = HY_TB * (dl + HY_NB)
            rhs = jnp.concatenate([gsh[:, x0:x0 + HY_TB], gsh[:, x0 - LANES:x0 - LANES + HY_TB]], axis=0)
            b0, b1 = max(0, -dl), min(HY_NB, HY_NB - dl)
            acc[(b0 + dl) * nb:(b1 + dl) * nb, :] += jnp.dot(ust[b0 * nb:b1 * nb, :], rhs, preferred_element_type=F32)
        y = jnp.concatenate([acc[tb * nb:(tb + 1) * nb, :] for tb in range(HY_NB)], axis=1)
        return y + d_ref[order, c0 + cc] * u

    v = [dwconv(pv_ref, cc, c0 + cc) for cc in chans]
    x1 = [dwconv(px1_ref, cc, HY_WIDTH + c0 + cc) for cc in chans]
    x2 = [dwconv(px2_ref, cc, 2 * HY_WIDTH + c0 + cc) for cc in chans]
    z = [x1[cc] * longconv(cc, 0, v[cc]) for cc in chans]
    for cc in chans:
        o_ref[cc] = (x2[cc] * longconv(cc, 1, z[cc])).astype(BF16)


def _hyena(pt, gt, conv_w, conv_b, dbias):
    nb = pt.shape[1]
    smem = pl.BlockSpec(memory_space=pltpu.SMEM)
    per_group = HY_WIDTH // HY_CH
    chan = lambda group: pl.BlockSpec((HY_CH, nb, SEQ), lambda c: (c + group * per_group, 0, 0))
    filt = lambda group: pl.BlockSpec((HY_CH, 1, 2 * SEQ), lambda c: (c + group * per_group, 0, 0))
    gt3 = gt.reshape(2 * HY_WIDTH, 1, 2 * SEQ)
    return pl.pallas_call(
        _hyena_kernel, grid=(per_group,),
        in_specs=[smem, smem, smem, chan(0), chan(1), chan(2), filt(0), filt(1)],
        out_specs=pl.BlockSpec((HY_CH, nb, SEQ), lambda c: (c, 0, 0)),
        out_shape=jax.ShapeDtypeStruct((HY_WIDTH, nb, SEQ), BF16),
        scratch_shapes=[pltpu.VMEM((2 * HY_CH, LANES, 2 * SEQ), BF16), pltpu.VMEM((HY_CH, HY_NB * nb, HY_TB), BF16),
                        pltpu.VMEM((HY_CH, HY_NB * nb, HY_TB), F32)],
        compiler_params=_params(), name="hyena",
    )(conv_w, conv_b, dbias, pt, pt, pt, gt3, gt3)


OP_TM = 1024
OP_RB = 512


def _outproj_kernel(oa_ref, hy_ref, oc_ref, w_ref, x_ref, g_ref, out_ref):
    def sub(i, carry):
        rows = pl.ds(pl.multiple_of(i * OP_RB, OP_RB), OP_RB)
        mixed = jnp.concatenate([oa_ref[rows, :], hy_ref[rows, :], oc_ref[rows, :]], axis=1)
        mix = jnp.dot(mixed, w_ref[...], preferred_element_type=F32)
        out_ref[rows, :] = x_ref[rows, :] + _rms(mix, g_ref[...])
        return carry

    lax.fori_loop(0, OP_TM // OP_RB, sub, 0)


def _outproj(oa, hyo, oc, w, x2, g):
    n = x2.shape[0]
    row = lambda width: pl.BlockSpec((OP_TM, width), lambda i: (i, 0))
    return pl.pallas_call(
        _outproj_kernel, grid=(n // OP_TM,),
        in_specs=[row(A_WIDTH), row(HY_WIDTH), row(C_WIDTH),
                  _resident((D_MODEL, D_MODEL)), row(D_MODEL), _resident((1, D_MODEL))],
        out_specs=row(D_MODEL),
        out_shape=jax.ShapeDtypeStruct((n, D_MODEL), F32),
        compiler_params=_params(), name="outproj",
    )(oa, hyo, oc, w, x2, g)


FF_TM = 512
FF_RB = 512
FF_HALO = 16


def _ffn_kernel(x_ref, xp_ref, xn_ref, gpre_ref, wg_ref, wu_ref, cw_ref, cb_ref, wd_ref, gpost_ref, out_ref, xs_ref):
    i = pl.program_id(0)
    tiles_per_seq = SEQ // FF_TM
    first = (i % tiles_per_seq) == 0
    last = (i % tiles_per_seq) == tiles_per_seq - 1
    xs_ref[:FF_HALO, :] = jnp.where(first, 0.0, xp_ref[...])
    xs_ref[FF_HALO:FF_HALO + FF_TM, :] = x_ref[...]
    xs_ref[FF_HALO + FF_TM:, :] = jnp.where(last, 0.0, xn_ref[...])

    def sub(s, carry):
        r0 = pl.multiple_of(s * FF_RB, FF_RB)
        xh = xs_ref[pl.ds(r0, FF_RB + 2 * FF_HALO), :]
        h = _rms(xh, gpre_ref[...]).astype(BF16)
        gate = jnp.dot(h, wg_ref[...], preferred_element_type=F32)
        up = jnp.dot(h[FF_HALO:FF_HALO + FF_RB], wu_ref[...], preferred_element_type=F32)
        gc = (gate[FF_HALO - 1:FF_HALO - 1 + FF_RB] * cw_ref[0:1, :]
              + gate[FF_HALO:FF_HALO + FF_RB] * cw_ref[1:2, :]
              + gate[FF_HALO + 1:FF_HALO + 1 + FF_RB] * cw_ref[2:3, :] + cb_ref[...])
        act = (jax.nn.gelu(gc, approximate=True) * up).astype(BF16)
        f = jnp.dot(act, wd_ref[...], preferred_element_type=F32)
        x = xh[FF_HALO:FF_HALO + FF_RB]
        out_ref[pl.ds(r0, FF_RB), :] = x + _rms(f, gpost_ref[...])
        return carry

    lax.fori_loop(0, FF_TM // FF_RB, sub, 0)


def _ffn(x2, gpre, wg, wu, cw, cb, wd, gpost):
    n = x2.shape[0]
    per = FF_TM // FF_HALO
    last_blk = n // FF_HALO - 1
    return pl.pallas_call(
        _ffn_kernel, grid=(n // FF_TM,),
        in_specs=[pl.BlockSpec((FF_TM, D_MODEL), lambda i: (i, 0)),
                  pl.BlockSpec((FF_HALO, D_MODEL), lambda i: (jnp.maximum(i * per - 1, 0), 0)),
                  pl.BlockSpec((FF_HALO, D_MODEL), lambda i: (jnp.minimum((i + 1) * per, last_blk), 0)),
                  _resident((1, D_MODEL)), _resident((D_MODEL, D_FF)), _resident((D_MODEL, D_FF)),
                  _resident((3, D_FF)), _resident((1, D_FF)), _resident((D_FF, D_MODEL)), _resident((1, D_MODEL))],
        out_specs=pl.BlockSpec((FF_TM, D_MODEL), lambda i: (i, 0)),
        out_shape=jax.ShapeDtypeStruct((n, D_MODEL), F32),
        scratch_shapes=[pltpu.VMEM((FF_TM + 2 * FF_HALO, D_MODEL), F32)],
        compiler_params=_params(), name="ffn",
    )(x2, x2, x2, gpre, wg, wu, cw, cb, wd, gpost)


def kernel(x, g_mix_pre, g_mix_post, g_ffn_pre, g_ffn_post, w_in, w_out, g_q, g_k, hy_conv_w, hy_conv_b, hy_w1, hy_b1, hy_freq, hy_w2, hy_b2, hy_w3, hy_decay, hy_d, ffn_w_gate, ffn_w_up, ffn_conv_w, ffn_conv_b, ffn_w_down):
    batch = x.shape[0]
    n = batch * SEQ
    tables = _rope_tables()
    bd = _head_mean_matrix()
    zt, trow = _hyena_positions()
    rowv = lambda a: a.reshape(1, -1).astype(F32)
    x2 = x.reshape(n, D_MODEL)
    for i in range(DEPTH):
        qa, ka, va, hy, qc, kc, vc = _inproj(
            x2, rowv(g_mix_pre[i]), w_in[i].astype(BF16), tables,
            rowv(jnp.tile(g_q[i], 2)), rowv(jnp.tile(g_k[i], 2)), bd)
        oa = _dilated(qa, ka, va, batch)
        oc = _gqa(qc, kc, vc, batch)
        gt = _filters(zt, trow, hy_w1[i], hy_b1[i], hy_freq[i], hy_w2[i], hy_b2[i], hy_w3[i], hy_decay[i])
        pt = jnp.transpose(hy.reshape(batch, SEQ, 3 * HY_WIDTH), (2, 0, 1))
        ot = _hyena(pt, gt, hy_conv_w[i].astype(F32), hy_conv_b[i].astype(F32), hy_d[i].astype(F32))
        hyo = jnp.transpose(ot, (1, 2, 0)).reshape(n, HY_WIDTH)
        x2 = _outproj(oa, hyo, oc, w_out[i].astype(BF16), x2, rowv(g_mix_post[i]))
        x2 = _ffn(x2, rowv(g_ffn_pre[i]), ffn_w_gate[i].astype(BF16), ffn_w_up[i].astype(BF16),
                  ffn_conv_w[i].astype(F32), rowv(ffn_conv_b[i]), ffn_w_down[i].astype(BF16),
                  rowv(g_ffn_post[i]))
    return x2.reshape(batch, SEQ, D_MODEL)
```

```python
import functools
import math

import numpy as np
import jax
import jax.numpy as jnp
from jax import lax
from jax.experimental import pallas as pl
from jax.experimental.pallas import tpu as pltpu

F32 = jnp.float32
BF16 = jnp.bfloat16

D_MODEL = 1024
SEQ = 2048
DEPTH = 2
HEAD_DIM = 64
A_WIDTH = 384
HY_WIDTH = 256
C_WIDTH = 384
C_KV_WIDTH = 128
PROJ_WIDTH = 2560
GRID_W = 64
ROPE_THETA = 10000.0
HY_BANDS = 16
HY_EMB = 33
HY_HIDDEN = 64
D_FF = 2816
EPS = 1e-6
N_SIDE = 64
QSCALE = HEAD_DIM ** -0.5 * math.log2(math.e)
NEG = -1e30

LANES = 128
VMEM_LIMIT = 56 * 1024 * 1024

HIGHEST = lax.Precision.HIGHEST


def _params(n_grid_dims=1):
    return pltpu.CompilerParams(
        dimension_semantics=("arbitrary",) * n_grid_dims, vmem_limit_bytes=VMEM_LIMIT)


def _resident(shape):
    nd = len(shape)
    return pl.BlockSpec(shape, lambda *_: (0,) * nd, pipeline_mode=pl.Buffered(1))


def _rope_tables():
    def angles(pos, dim):
        freqs = ROPE_THETA ** (-np.arange(0, dim, 2, dtype=np.float64) / dim)
        ang = pos.astype(np.float64)[:, None] * freqs[None, :]
        return np.cos(ang), np.sin(ang)

    pos = np.arange(SEQ)
    c, s = angles(pos, HEAD_DIM)
    cos_a = np.tile(np.concatenate([c, c], -1), (1, 2))
    sin_a = np.tile(np.concatenate([-s, s], -1), (1, 2))
    cr, sr = angles(pos // GRID_W, HEAD_DIM // 2)
    cc, sc = angles(pos % GRID_W, HEAD_DIM // 2)
    cos_c = np.tile(np.concatenate([cr, cr, cc, cc], -1), (1, 2))
    sin_c = np.tile(np.concatenate([-sr, sr, -sc, sc], -1), (1, 2))
    return tuple(jnp.asarray(t, F32) for t in (cos_a, sin_a, cos_c, sin_c))


def _head_mean_matrix():
    m = np.kron(np.eye(LANES // HEAD_DIM), np.full((HEAD_DIM, HEAD_DIM), 1.0 / HEAD_DIM))
    return jnp.asarray(np.concatenate([m, m], 0), BF16)


def _hyena_positions():
    L = SEQ
    t = np.linspace(0.0, 1.0, L)
    bands = np.linspace(1e-4, HY_BANDS - 1, HY_BANDS)
    ang = 2.0 * math.pi * bands[None, :] * np.arange(L)[:, None] / L
    z = np.concatenate([t[:, None], np.cos(ang), -np.sin(ang)], -1)
    zt = np.zeros((LANES, 2 * L))
    zt[:HY_EMB, :L] = z[::-1].T
    zt[:HY_EMB, L:] = z.T
    trow = np.concatenate([t[::-1], t])[None, :]
    return jnp.asarray(zt, F32), jnp.asarray(trow, F32)


def _rope(x, cos, sin_signed, half, lane):
    first = (lane & (2 * half - 1)) < half
    swapped = jnp.where(first, pltpu.roll(x, LANES - half, 1), pltpu.roll(x, half, 1))
    return x * cos + swapped * sin_signed


def _rms(x, gain):
    ms = jnp.mean(x * x, axis=-1, keepdims=True)
    return x * lax.rsqrt(ms + EPS) * gain


IN_TM = 2048
IN_RB = 512
IN_STEPS = IN_TM // IN_RB


def _inproj_kernel(x_ref, g_ref, w_ref, cosa_ref, sina_ref, cosc_ref, sinc_ref, gq_ref, gk_ref, bd_ref,
                   qa_ref, ka_ref, va_ref, hy_ref, qc_ref, kc_ref, vc_ref, h_ref):
    lane = lax.broadcasted_iota(jnp.int32, (IN_RB, LANES), 1)

    def normed(i):
        rows = pl.ds(pl.multiple_of(i * IN_RB, IN_RB), IN_RB)
        return _rms(x_ref[rows, :], g_ref[...]).astype(BF16)

    h_ref[0] = normed(0)

    def head_norm(blocks, gains):
        hi_lo = jnp.concatenate([jnp.concatenate(_split_hi_lo(v * v), axis=1) for v in blocks], axis=0)
        ms = jnp.dot(hi_lo, bd_ref[...], preferred_element_type=F32)
        return [v * lax.rsqrt(ms[i * IN_RB:(i + 1) * IN_RB] + EPS) * gain
                for i, (v, gain) in enumerate(zip(blocks, gains))]

    def sub(i, carry):
        rows = pl.ds(pl.multiple_of(i * IN_RB, IN_RB), IN_RB)
        slot = i % 2
        h = h_ref[slot]
        h_ref[1 - slot] = normed(jnp.minimum(i + 1, IN_STEPS - 1))
        cosa, sina = cosa_ref[rows, :], sina_ref[rows, :]
        cosc, sinc = cosc_ref[rows, :], sinc_ref[rows, :]
        p = jnp.dot(h, w_ref[...], preferred_element_type=F32)
        block = lambda gb: p[:, gb * LANES:(gb + 1) * LANES]
        normed_qk = head_norm([block(gb) for gb in range(15, 19)], [gq_ref[...]] * 3 + [gk_ref[...]])
        for gb in range(PROJ_WIDTH // LANES):
            blk = block(gb)
            if gb < 3:
                cols = slice(gb * LANES, (gb + 1) * LANES)
                qa_ref[rows, cols] = (_rope(blk, cosa, sina, 32, lane) * QSCALE).astype(BF16)
            elif gb < 6:
                cols = slice((gb - 3) * LANES, (gb - 2) * LANES)
                ka_ref[rows, cols] = _rope(blk, cosa, sina, 32, lane).astype(BF16)
            elif gb < 9:
                cols = slice((gb - 6) * LANES, (gb - 5) * LANES)
                va_ref[rows, cols] = blk.astype(BF16)
            elif gb < 15:
                cols = slice((gb - 9) * LANES, (gb - 8) * LANES)
                hy_ref[rows, cols] = blk.astype(BF16)
            elif gb < 18:
                cols = slice((gb - 15) * LANES, (gb - 14) * LANES)
                q = _rope(normed_qk[gb - 15], cosc, sinc, 16, lane)
                qc_ref[rows, cols] = (q * QSCALE).astype(BF16)
            elif gb == 18:
                kc_ref[rows, :] = _rope(normed_qk[3], cosc, sinc, 16, lane).astype(BF16)
            else:
                vc_ref[rows, :] = blk.astype(BF16)
        return carry

    lax.fori_loop(0, IN_STEPS, sub, 0)


def _inproj(x2, g, w, tables, gq, gk, bd):
    n = x2.shape[0]
    tiles_per_seq = SEQ // IN_TM
    row = lambda width: pl.BlockSpec((IN_TM, width), lambda i: (i, 0))
    tab = pl.BlockSpec((IN_TM, LANES), lambda i: (i % tiles_per_seq, 0))
    outs = [A_WIDTH, A_WIDTH, A_WIDTH, 3 * HY_WIDTH, C_WIDTH, C_KV_WIDTH, C_KV_WIDTH]
    return pl.pallas_call(
        _inproj_kernel,
        grid=(n // IN_TM,),
        in_specs=[row(D_MODEL), _resident((1, D_MODEL)), _resident((D_MODEL, PROJ_WIDTH)),
                  tab, tab, tab, tab, _resident((1, LANES)), _resident((1, LANES)), _resident((2 * LANES, LANES))],
        out_specs=[row(wd) for wd in outs],
        out_shape=[jax.ShapeDtypeStruct((n, wd), BF16) for wd in outs],
        scratch_shapes=[pltpu.VMEM((2, IN_RB, D_MODEL), BF16)],
        compiler_params=_params(),
        name="inproj",
    )(x2, g, w, *tables, gq, gk, bd)


DL_BQ = 128
DL_BK = 256
DL_PER_STEP = 2
DL_BLK = 256
DL_NBLK = SEQ // DL_BLK
DL_NPROB = SEQ // DL_BQ


DL_R = 16
DL_SEG = SEQ // DL_R


def _stat_lane(h):
    return HEAD_DIM * (h % 2) + h // 2


def _dilated_constants():
    per = DL_BLK // DL_R
    p = np.zeros((DL_BLK, DL_BLK))
    for i in range(per):
        for r in range(DL_R):
            p[r * per + i, DL_R * i + r] = 1.0
    e = np.zeros((LANES, A_WIDTH))
    for h in range(A_WIDTH // HEAD_DIM):
        e[_stat_lane(h), h * HEAD_DIM:(h + 1) * HEAD_DIM] = 1.0
    return jnp.asarray(p, BF16), jnp.asarray(p.T, BF16), jnp.asarray(np.concatenate([e, e], 0), BF16)


def _band_problems(probs, lane):
    lo = lax.broadcasted_iota(jnp.int32, (1, LANES), 1) < HEAD_DIM
    npair = A_WIDTH // LANES
    scores = []
    for q, k, _, _ in probs:
        nq = q.shape[0]
        ss = []
        for j in range(npair):
            cols = slice(j * LANES, (j + 1) * LANES)
            qj, zero = q[:, cols], jnp.zeros_like(q[:, cols])
            qst = jnp.concatenate([jnp.where(lo, qj, zero), jnp.where(lo, zero, qj)], axis=0)
            s2 = lax.dot_general(qst, k[:, cols], (((1,), (1,)), ((), ())), preferred_element_type=F32)
            ss += [s2[:nq], s2[nq:]]
        scores.append(ss)
    results = []
    for (q, _, v, mask), ss in zip(probs, scores):
        nq, nk = q.shape[0], v.shape[0]
        ones = jnp.ones((nk, LANES), BF16)
        ps = []
        mtile = jnp.zeros((nq, LANES), F32)
        for h, s in enumerate(ss):
            s = s + mask
            m = jnp.max(s, axis=-1, keepdims=True)
            ps.append(jnp.exp2(s - m).astype(BF16))
            mtile = jnp.where(lane == _stat_lane(h), m, mtile)
        outs = []
        ltile = jnp.ones((nq, LANES), F32)
        for j in range(npair):
            vext = jnp.concatenate([v[:, j * LANES:(j + 1) * LANES], ones], axis=1)
            d = jnp.dot(jnp.concatenate(ps[2 * j:2 * j + 2], axis=0), vext, preferred_element_type=F32)
            num0, den0, num1, den1 = d[:nq, :LANES], d[:nq, LANES:], d[nq:, :LANES], d[nq:, LANES:]
            outs.append(jnp.where(lo, num0 * (1.0 / den0), num1 * (1.0 / den1)))
            ltile = jnp.where(lane == j, den0, jnp.where(lane == HEAD_DIM + j, den1, ltile))
        results.append((jnp.concatenate(outs, axis=1), mtile + jnp.log2(ltile)))
    return results


def _split_hi_lo(x):
    hi = x.astype(BF16)
    return hi, (x - hi.astype(F32)).astype(BF16)


def _dilated_kernel(q_ref, k_ref, v_ref, pf_ref, pb_ref, ex_ref, out_ref, xp_ref, ob_ref, l1_ref, st_ref):
    lane = lax.broadcasted_iota(jnp.int32, (DL_BQ, LANES), 1)
    per = DL_BLK // DL_R

    def block_chunks(blk):
        return [pl.ds(pl.multiple_of(r * DL_SEG + blk * per, per), per) for r in range(DL_R)]

    def regroup(blk, carry):
        rows = pl.ds(pl.multiple_of(blk * DL_BLK, DL_BLK), DL_BLK)
        x = jnp.concatenate([q_ref[rows, :], k_ref[rows, :], v_ref[rows, :]], axis=1)
        y = jnp.dot(pf_ref[...], x, preferred_element_type=F32).astype(BF16)
        for r, at in enumerate(block_chunks(blk)):
            for a in range(3):
                xp_ref[a, at, :] = y[r * per:(r + 1) * per, a * A_WIDTH:(a + 1) * A_WIDTH]
        return carry

    lax.fori_loop(0, DL_NBLK, regroup, 0)

    def run(n_problems, per_step, gather, scatter):
        def step(i, carry):
            ids = [i * per_step + u for u in range(per_step)]
            for p, (o, lse) in zip(ids, _band_problems([gather(p) for p in ids], lane)):
                scatter(p, o.astype(BF16), lse)
            return carry

        lax.fori_loop(0, n_problems // per_step, step, 0)

    def band_mask(diff):
        return jnp.where((diff <= N_SIDE) & (diff >= -N_SIDE), 0.0, NEG)

    def put_regrouped(b, rows, o, lse):
        hi, lo = _split_hi_lo(lse)
        ob_ref[b, rows, :] = o
        st_ref[b - 1, rows, :LANES] = hi
        st_ref[b - 1, rows, LANES:] = lo

    row = lax.broadcasted_iota(jnp.int32, (DL_BQ, DL_BK), 0)
    col = lax.broadcasted_iota(jnp.int32, (DL_BQ, DL_BK), 1)

    def gather1(p):
        t0 = pl.multiple_of(p * DL_BQ, DL_BQ)
        ks = pl.multiple_of(jnp.clip(t0 - N_SIDE, 0, SEQ - DL_BK), N_SIDE)
        keys = pl.ds(ks, DL_BK)
        return q_ref[pl.ds(t0, DL_BQ), :], k_ref[keys, :], v_ref[keys, :], band_mask(col - row + (ks - t0))

    def scatter1(p, o, lse):
        rows = pl.ds(pl.multiple_of(p * DL_BQ, DL_BQ), DL_BQ)
        ob_ref[0, rows, :] = o
        l1_ref[rows, :] = lse

    run(DL_NPROB, DL_PER_STEP, gather1, scatter1)

    n4, qrows, krows = 4, DL_BQ // 4, DL_BK // 4
    slab_of = lambda idx, n: lax.shift_right_logical(idx, n.bit_length() - 1)
    ddiff = 4 * ((col & (krows - 1)) - (row & (qrows - 1))) + (slab_of(col, krows) - slab_of(row, qrows))

    def gather4(p):
        c, b = p // n4, p % n4
        q0 = pl.multiple_of(b * qrows, qrows)
        ks = pl.multiple_of(jnp.clip(q0 - N_SIDE // 4, 0, DL_SEG - krows), N_SIDE // 4)
        slabs = [(c + 4 * s) * DL_SEG for s in range(4)]
        take = lambda a, start, n: jnp.concatenate([xp_ref[a, pl.ds(s0 + start, n), :] for s0 in slabs], axis=0)
        return (take(0, q0, qrows), take(1, ks, krows), take(2, ks, krows), band_mask(ddiff + 4 * (ks - q0)))

    def scatter4(p, o, lse):
        c, b = p // n4, p % n4
        for s in range(4):
            rows = pl.ds(pl.multiple_of((c + 4 * s) * DL_SEG + b * qrows, qrows), qrows)
            put_regrouped(1, rows, o[s * qrows:(s + 1) * qrows], lse[s * qrows:(s + 1) * qrows])

    run(DL_NPROB, DL_PER_STEP, gather4, scatter4)

    mask16 = band_mask(lax.broadcasted_iota(jnp.int32, (DL_BQ, DL_SEG), 1)
                       - lax.broadcasted_iota(jnp.int32, (DL_BQ, DL_SEG), 0))

    def gather16(p):
        rows = pl.ds(pl.multiple_of(p * DL_SEG, DL_SEG), DL_SEG)
        return xp_ref[0, rows, :], xp_ref[1, rows, :], xp_ref[2, rows, :], mask16

    def scatter16(p, o, lse):
        put_regrouped(2, pl.ds(pl.multiple_of(p * DL_SEG, DL_SEG), DL_SEG), o, lse)

    run(DL_R, 2 * DL_PER_STEP, gather16, scatter16)

    def merge(blk, carry):
        rows = pl.ds(pl.multiple_of(blk * DL_BLK, DL_BLK), DL_BLK)
        z = jnp.concatenate(
            [jnp.concatenate([ob_ref[1, at, :], st_ref[0, at, :], ob_ref[2, at, :], st_ref[1, at, :]], axis=1)
             for at in block_chunks(blk)], axis=0)
        nat = jnp.dot(pb_ref[...], z, preferred_element_type=F32)
        w0 = A_WIDTH + 2 * LANES
        o4, s4 = nat[:, :A_WIDTH], nat[:, A_WIDTH:A_WIDTH + LANES] + nat[:, A_WIDTH + LANES:w0]
        o16, s16 = nat[:, w0:w0 + A_WIDTH], nat[:, w0 + A_WIDTH:w0 + A_WIDTH + LANES] + nat[:, w0 + A_WIDTH + LANES:]
        o1, s1 = ob_ref[0, rows, :].astype(F32), l1_ref[rows, :]
        mx = jnp.maximum(jnp.maximum(s1, s4), s16)
        es = [jnp.exp2(s - mx) for s in (s1, s4, s16)]
        inv = 1.0 / (es[0] + es[1] + es[2])
        spread = lambda w: jnp.dot(jnp.concatenate(_split_hi_lo(w), axis=1), ex_ref[...], preferred_element_type=F32)
        w1, w4 = spread(es[0] * inv), spread(es[1] * inv)
        out_ref[rows, :] = (w1 * o1 + w4 * o4 + (1.0 - w1 - w4) * o16).astype(BF16)
        return carry

    lax.fori_loop(0, DL_NBLK, merge, 0)


def _dilated(qa, ka, va, batch):
    seq = pl.BlockSpec((None, SEQ, A_WIDTH), lambda b: (b, 0, 0))
    pf, pb, ex = _dilated_constants()
    shaped = lambda a: a.reshape(batch, SEQ, A_WIDTH)
    out = pl.pallas_call(
        _dilated_kernel, grid=(batch,),
        in_specs=[seq, seq, seq, _resident(pf.shape), _resident(pb.shape), _resident(ex.shape)],
        out_specs=seq,
        out_shape=jax.ShapeDtypeStruct((batch, SEQ, A_WIDTH), BF16),
        scratch_shapes=[pltpu.VMEM((3, SEQ, A_WIDTH), BF16), pltpu.VMEM((3, SEQ, A_WIDTH), BF16),
                        pltpu.VMEM((SEQ, LANES), F32), pltpu.VMEM((2, SEQ, 2 * LANES), BF16)],
        compiler_params=_params(), name="dilated",
    )(shaped(qa), shaped(ka), shaped(va), pf, pb, ex)
    return out.reshape(batch * SEQ, A_WIDTH)


GQ_TQ = 512


def _gqa_kernel(q_ref, k_ref, v_ref, o_ref, ksw_ref, vm_ref):
    lane = lax.broadcasted_iota(jnp.int32, (SEQ, LANES), 1)
    lo = lane < HEAD_DIM
    k = k_ref[...]
    v = v_ref[...]
    ksw_ref[...] = pltpu.roll(k.astype(F32), HEAD_DIM, 1).astype(BF16)
    vsw = pltpu.roll(v.astype(F32), HEAD_DIM, 1).astype(BF16)
    zero = jnp.zeros_like(v)
    vm_ref[0, :, :LANES] = jnp.where(lo, v, zero)
    vm_ref[1, :, :LANES] = jnp.where(lo, zero, vsw)
    vm_ref[2, :, :LANES] = jnp.where(lo, vsw, zero)
    vm_ref[3, :, :LANES] = jnp.where(lo, zero, v)
    for idx in range(4):
        vm_ref[idx, :, LANES:] = jnp.ones((SEQ, LANES), BF16)
    lo_q = lax.broadcasted_iota(jnp.int32, (1, LANES), 1) < HEAD_DIM

    nh = 2 * (C_WIDTH // LANES)

    def blk(i, carry):
        rows = pl.ds(pl.multiple_of(i * GQ_TQ, GQ_TQ), GQ_TQ)

        def scores(h):
            j, half = divmod(h, 2)
            qj = q_ref[rows, j * LANES:(j + 1) * LANES]
            sel = lo_q if half == 0 else jnp.logical_not(lo_q)
            qm = jnp.where(sel, qj, jnp.zeros_like(qj))
            kk = k_ref[...] if h // 3 == half else ksw_ref[...]
            return lax.dot_general(qm, kk, (((1,), (1,)), ((), ())), preferred_element_type=F32)

        def softmax(s):
            return jnp.exp2(s - jnp.max(s, axis=-1, keepdims=True)).astype(BF16)

        def values(h, pb):
            d = jnp.dot(pb, vm_ref[2 * (h // 3) + h % 2], preferred_element_type=F32)
            return d[:, :LANES] * (1.0 / d[:, LANES:])

        pending = {0: scores(0), 1: scores(1)}
        outs = []
        for h in range(nh):
            pb = softmax(pending.pop(h))
            if h + 2 < nh:
                pending[h + 2] = scores(h + 2)
            outs.append(values(h, pb))
        for j in range(nh // 2):
            o_ref[rows, j * LANES:(j + 1) * LANES] = (outs[2 * j] + outs[2 * j + 1]).astype(BF16)
        return carry

    lax.fori_loop(0, SEQ // GQ_TQ, blk, 0)


def _gqa(qc, kc, vc, batch):
    spec = lambda width: pl.BlockSpec((None, SEQ, width), lambda b: (b, 0, 0))
    out = pl.pallas_call(
        _gqa_kernel, grid=(batch,),
        in_specs=[spec(C_WIDTH), spec(C_KV_WIDTH), spec(C_KV_WIDTH)],
        out_specs=spec(C_WIDTH),
        out_shape=jax.ShapeDtypeStruct((batch, SEQ, C_WIDTH), BF16),
        scratch_shapes=[pltpu.VMEM((SEQ, LANES), BF16), pltpu.VMEM((4, SEQ, 2 * LANES), BF16)],
        compiler_params=_params(), name="gqa",
    )(qc.reshape(batch, SEQ, C_WIDTH), kc.reshape(batch, SEQ, C_KV_WIDTH), vc.reshape(batch, SEQ, C_KV_WIDTH))
    return out.reshape(batch * SEQ, C_WIDTH)


def _filter_kernel(z_ref, trow_ref, w1t_ref, b1_ref, f0_ref, w2t_ref, b2_ref, f1_ref, w3t_ref, dec_ref, out_ref):
    L = SEQ
    dot = functools.partial(jnp.dot, precision=HIGHEST, preferred_element_type=F32)
    h = jnp.sin(f0_ref[...] * (dot(w1t_ref[...], z_ref[...]) + b1_ref[...]))
    h = jnp.sin(f1_ref[...] * (dot(w2t_ref[...], h) + b2_ref[...]))
    w3t = w3t_ref[...]
    dec = dec_ref[...]
    trow = trow_ref[...]
    hf = dot(w3t[:HY_WIDTH], h[:, L:]) * jnp.exp(-trow[:, L:] * dec[:HY_WIDTH])
    hb = dot(w3t[HY_WIDTH:], h[:, :L]) * jnp.exp(-trow[:, :L] * dec[HY_WIDTH:])
    col = lax.broadcasted_iota(jnp.int32, hb.shape, 1)
    hb = jnp.where(col == 0, 0.0, pltpu.roll(hb, 1, 1))
    norm = (jnp.sum(jnp.abs(hf), axis=-1, keepdims=True) + jnp.sum(jnp.abs(hb), axis=-1, keepdims=True))
    out_ref[:, :L] = hb / norm
    out_ref[:, L:] = hf / norm


def _filters(zt, trow, w1, b1, freq, w2, b2, w3, decay):
    colv = lambda a: a.reshape(-1, 1).astype(F32)
    w1t = jnp.zeros((HY_HIDDEN, LANES), F32).at[:, :HY_EMB].set(w1.T)
    order = pl.BlockSpec((2 * HY_WIDTH, HY_HIDDEN), lambda o: (o, 0))
    return pl.pallas_call(
        _filter_kernel, grid=(2,),
        in_specs=[_resident(zt.shape), _resident(trow.shape), _resident(w1t.shape), _resident((HY_HIDDEN, 1)),
                  _resident((HY_HIDDEN, 1)), _resident((HY_HIDDEN, HY_HIDDEN)), _resident((HY_HIDDEN, 1)),
                  _resident((HY_HIDDEN, 1)), order, pl.BlockSpec((2 * HY_WIDTH, 1), lambda o: (o, 0))],
        out_specs=pl.BlockSpec((HY_WIDTH, 2 * SEQ), lambda o: (o, 0)),
        out_shape=jax.ShapeDtypeStruct((2 * HY_WIDTH, 2 * SEQ), F32),
        compiler_params=_params(), name="filters",
    )(zt, trow, w1t, colv(b1), colv(freq[0]), w2.T, colv(b2), colv(freq[1]), w3.T, colv(decay))


HY_TB = 256
HY_NB = SEQ // HY_TB


HY_CH = 4


def _hyena_kernel(cw_ref, cb_ref, d_ref, pv_ref, px1_ref, px2_ref, g0_ref, g1_ref, o_ref, gsh_ref, ust_ref, acc_ref):
    c0 = pl.program_id(0) * HY_CH
    nb = pv_ref.shape[1]
    col = lax.broadcasted_iota(jnp.int32, (nb, SEQ), 1)
    chans = range(HY_CH)

    def build_strip(cc, order):
        g = jnp.broadcast_to((g0_ref, g1_ref)[order][cc], (16, 2 * SEQ))
        base = pltpu.bitcast(pltpu.roll(g, 0, 1, stride=1, stride_axis=0).astype(BF16), jnp.uint32)
        for a in range(LANES // 16):
            rows = base if a == 0 else pltpu.roll(base, 16 * a, 1)
            gsh_ref[2 * cc + order, 16 * a:16 * (a + 1), :] = pltpu.bitcast(rows, BF16)

    def dwconv(p_ref, cc, ch):
        p = p_ref[cc].astype(F32)
        prev = jnp.where(col == 0, 0.0, pltpu.roll(p, 1, 1))
        nxt = jnp.where(col == SEQ - 1, 0.0, pltpu.roll(p, SEQ - 1, 1))
        return prev * cw_ref[0, ch] + p * cw_ref[1, ch] + nxt * cw_ref[2, ch] + cb_ref[ch]

    def longconv(cc, order, u):
        gsh = gsh_ref.at[2 * cc + order]
        ust, acc = ust_ref.at[cc], acc_ref.at[cc]
        ub = u.astype(BF16)
        for tb in range(HY_NB):
            ust[tb * nb:(tb + 1) * nb, :] = ub[:, tb * HY_TB:(tb + 1) * HY_TB]
        acc[...] = jnp.zeros(acc.shape, F32)
        for dl in range(-(HY_NB - 1), HY_NB):
            x0 = HY_TB * (dl + HY_NB)
            rhs = jnp.concatenate([gsh[:, x0:x0 + HY_TB], gsh[:, x0 - LANES:x0 - LANES + HY_TB]], axis=0)
            b0, b1 = max(0, -dl), min(HY_NB, HY_NB - dl)
            acc[(b0 + dl) * nb:(b1 + dl) * nb, :] += jnp.dot(ust[b0 * nb:b1 * nb, :], rhs, preferred_element_type=F32)
        y = jnp.concatenate([acc[tb * nb:(tb + 1) * nb, :] for tb in range(HY_NB)], axis=1)
        return y + d_ref[order, c0 + cc] * u

    y1, y2 = [], []
    for cc in chans:
        build_strip(cc, 0)
        y1.append(longconv(cc, 0, dwconv(pv_ref, cc, c0 + cc)))
    for cc in chans:
        build_strip(cc, 1)
        y2.append(longconv(cc, 1, dwconv(px1_ref, cc, HY_WIDTH + c0 + cc) * y1[cc]))
    for cc in chans:
        o_ref[cc] = (dwconv(px2_ref, cc, 2 * HY_WIDTH + c0 + cc) * y2[cc]).astype(BF16)


def _hyena(pt, gt, conv_w, conv_b, dbias):
    nb = pt.shape[1]
    smem = pl.BlockSpec(memory_space=pltpu.SMEM)
    per_group = HY_WIDTH // HY_CH
    chan = lambda group: pl.BlockSpec((HY_CH, nb, SEQ), lambda c: (c + group * per_group, 0, 0))
    filt = lambda group: pl.BlockSpec((HY_CH, 1, 2 * SEQ), lambda c: (c + group * per_group, 0, 0))
    gt3 = gt.reshape(2 * HY_WIDTH, 1, 2 * SEQ)
    return pl.pallas_call(
        _hyena_kernel, grid=(per_group,),
        in_specs=[smem, smem, smem, chan(0), chan(1), chan(2), filt(0), filt(1)],
        out_specs=pl.BlockSpec((HY_CH, nb, SEQ), lambda c: (c, 0, 0)),
        out_shape=jax.ShapeDtypeStruct((HY_WIDTH, nb, SEQ), BF16),
        scratch_shapes=[pltpu.VMEM((2 * HY_CH, LANES, 2 * SEQ), BF16), pltpu.VMEM((HY_CH, HY_NB * nb, HY_TB), BF16),
                        pltpu.VMEM((HY_CH, HY_NB * nb, HY_TB), F32)],
        compiler_params=_params(), name="hyena",
    )(conv_w, conv_b, dbias, pt, pt, pt, gt3, gt3)


OP_TM = 2048
OP_RB = 512


def _outproj_kernel(oa_ref, hy_ref, oc_ref, w_ref, x_ref, g_ref, out_ref):
    def sub(i, carry):
        blocks = [pl.ds(pl.multiple_of((2 * i + u) * OP_RB, OP_RB), OP_RB) for u in range(2)]
        mixes = [jnp.dot(jnp.concatenate([oa_ref[rows, :], hy_ref[rows, :], oc_ref[rows, :]], axis=1), w_ref[...],
                         preferred_element_type=F32) for rows in blocks]
        for rows, mix in zip(blocks, mixes):
            out_ref[rows, :] = x_ref[rows, :] + _rms(mix, g_ref[...])
        return carry

    lax.fori_loop(0, OP_TM // (2 * OP_RB), sub, 0)


def _outproj(oa, hyo, oc, w, x2, g):
    n = x2.shape[0]
    row = lambda width: pl.BlockSpec((OP_TM, width), lambda i: (i, 0))
    return pl.pallas_call(
        _outproj_kernel, grid=(n // OP_TM,),
        in_specs=[row(A_WIDTH), row(HY_WIDTH), row(C_WIDTH),
                  _resident((D_MODEL, D_MODEL)), row(D_MODEL), _resident((1, D_MODEL))],
        out_specs=row(D_MODEL),
        out_shape=jax.ShapeDtypeStruct((n, D_MODEL), F32),
        compiler_params=_params(), name="outproj",
    )(oa, hyo, oc, w, x2, g)


FF_TM = 512
FF_RB = 256
FF_HALO = 16


def _ffn_kernel(x_ref, xp_ref, xn_ref, gpre_ref, wg_ref, wu_ref, cw_ref, cb_ref, wd_ref, gpost_ref, out_ref, xs_ref):
    i = pl.program_id(0)
    tiles_per_seq = SEQ // FF_TM
    first = (i % tiles_per_seq) == 0
    last = (i % tiles_per_seq) == tiles_per_seq - 1
    xs_ref[:FF_HALO, :] = jnp.where(first, 0.0, xp_ref[...])
    xs_ref[FF_HALO:FF_HALO + FF_TM, :] = x_ref[...]
    xs_ref[FF_HALO + FF_TM:, :] = jnp.where(last, 0.0, xn_ref[...])

    acts, xs = [], []
    for s in range(FF_TM // FF_RB):
        xh = xs_ref[s * FF_RB:s * FF_RB + FF_RB + 2 * FF_HALO, :]
        h = _rms(xh, gpre_ref[...]).astype(BF16)
        gate = jnp.dot(h, wg_ref[...], preferred_element_type=F32)
        up = jnp.dot(h[FF_HALO:FF_HALO + FF_RB], wu_ref[...], preferred_element_type=F32)
        gc = (gate[FF_HALO - 1:FF_HALO - 1 + FF_RB] * cw_ref[0:1, :]
              + gate[FF_HALO:FF_HALO + FF_RB] * cw_ref[1:2, :]
              + gate[FF_HALO + 1:FF_HALO + 1 + FF_RB] * cw_ref[2:3, :] + cb_ref[...])
        acts.append((jax.nn.gelu(gc, approximate=True) * up).astype(BF16))
        xs.append(xh[FF_HALO:FF_HALO + FF_RB])
    for s, (act, x) in enumerate(zip(acts, xs)):
        f = jnp.dot(act, wd_ref[...], preferred_element_type=F32)
        out_ref[s * FF_RB:(s + 1) * FF_RB, :] = x + _rms(f, gpost_ref[...])


def _ffn(x2, gpre, wg, wu, cw, cb, wd, gpost):
    n = x2.shape[0]
    per = FF_TM // FF_HALO
    last_blk = n // FF_HALO - 1
    return pl.pallas_call(
        _ffn_kernel, grid=(n // FF_TM,),
        in_specs=[pl.BlockSpec((FF_TM, D_MODEL), lambda i: (i, 0)),
                  pl.BlockSpec((FF_HALO, D_MODEL), lambda i: (jnp.maximum(i * per - 1, 0), 0)),
                  pl.BlockSpec((FF_HALO, D_MODEL), lambda i: (jnp.minimum((i + 1) * per, last_blk), 0)),
                  _resident((1, D_MODEL)), _resident((D_MODEL, D_FF)), _resident((D_MODEL, D_FF)),
                  _resident((3, D_FF)), _resident((1, D_FF)), _resident((D_FF, D_MODEL)), _resident((1, D_MODEL))],
        out_specs=pl.BlockSpec((FF_TM, D_MODEL), lambda i: (i, 0)),
        out_shape=jax.ShapeDtypeStruct((n, D_MODEL), F32),
        scratch_shapes=[pltpu.VMEM((FF_TM + 2 * FF_HALO, D_MODEL), F32)],
        compiler_params=_params(), name="ffn",
    )(x2, x2, x2, gpre, wg, wu, cw, cb, wd, gpost)


def kernel(x, g_mix_pre, g_mix_post, g_ffn_pre, g_ffn_post, w_in, w_out, g_q, g_k, hy_conv_w, hy_conv_b, hy_w1, hy_b1, hy_freq, hy_w2, hy_b2, hy_w3, hy_decay, hy_d, ffn_w_gate, ffn_w_up, ffn_conv_w, ffn_conv_b, ffn_w_down):
    batch = x.shape[0]
    n = batch * SEQ
    tables = _rope_tables()
    bd = _head_mean_matrix()
    zt, trow = _hyena_positions()
    rowv = lambda a: a.reshape(1, -1).astype(F32)
    x2 = x.reshape(n, D_MODEL)
    for i in range(DEPTH):
        qa, ka, va, hy, qc, kc, vc = _inproj(
            x2, rowv(g_mix_pre[i]), w_in[i].astype(BF16), tables,
            rowv(jnp.tile(g_q[i], 2)), rowv(jnp.tile(g_k[i], 2)), bd)
        oa = _dilated(qa, ka, va, batch)
        oc = _gqa(qc, kc, vc, batch)
        gt = _filters(zt, trow, hy_w1[i], hy_b1[i], hy_freq[i], hy_w2[i], hy_b2[i], hy_w3[i], hy_decay[i])
        pt = jnp.transpose(hy.reshape(batch, SEQ, 3 * HY_WIDTH), (2, 0, 1))
        ot = _hyena(pt, gt, hy_conv_w[i].astype(F32), hy_conv_b[i].astype(F32), hy_d[i].astype(F32))
        hyo = jnp.transpose(ot, (1, 2, 0)).reshape(n, HY_WIDTH)
        x2 = _outproj(oa, hyo, oc, w_out[i].astype(BF16), x2, rowv(g_mix_post[i]))
        x2 = _ffn(x2, rowv(g_ffn_pre[i]), ffn_w_gate[i].astype(BF16), ffn_w_up[i].astype(BF16),
                  ffn_conv_w[i].astype(F32), rowv(ffn_conv_b[i]), ffn_w_down[i].astype(BF16),
                  rowv(g_ffn_post[i]))
    return x2.reshape(batch, SEQ, D_MODEL)
```

```python
import functools
import math

import numpy as np
import jax
import jax.numpy as jnp
from jax import lax
from jax.experimental import pallas as pl
from jax.experimental.pallas import tpu as pltpu

F32 = jnp.float32
BF16 = jnp.bfloat16

D_MODEL = 1024
SEQ = 2048
DEPTH = 2
HEAD_DIM = 64
A_WIDTH = 384
HY_WIDTH = 256
C_WIDTH = 384
C_KV_WIDTH = 128
PROJ_WIDTH = 2560
GRID_W = 64
ROPE_THETA = 10000.0
HY_BANDS = 16
HY_EMB = 33
HY_HIDDEN = 64
D_FF = 2816
EPS = 1e-6
N_SIDE = 64
QSCALE = HEAD_DIM ** -0.5 * math.log2(math.e)
NEG = -1e30

LANES = 128
VMEM_LIMIT = 56 * 1024 * 1024

HIGHEST = lax.Precision.HIGHEST


def _params(n_grid_dims=1):
    return pltpu.CompilerParams(
        dimension_semantics=("arbitrary",) * n_grid_dims, vmem_limit_bytes=VMEM_LIMIT)


def _resident(shape):
    nd = len(shape)
    return pl.BlockSpec(shape, lambda *_: (0,) * nd, pipeline_mode=pl.Buffered(1))


def _rope_tables():
    def angles(pos, dim):
        freqs = ROPE_THETA ** (-np.arange(0, dim, 2, dtype=np.float64) / dim)
        ang = pos.astype(np.float64)[:, None] * freqs[None, :]
        return np.cos(ang), np.sin(ang)

    pos = np.arange(SEQ)
    c, s = angles(pos, HEAD_DIM)
    cos_a = np.tile(np.concatenate([c, c], -1), (1, 2))
    sin_a = np.tile(np.concatenate([-s, s], -1), (1, 2))
    cr, sr = angles(pos // GRID_W, HEAD_DIM // 2)
    cc, sc = angles(pos % GRID_W, HEAD_DIM // 2)
    cos_c = np.tile(np.concatenate([cr, cr, cc, cc], -1), (1, 2))
    sin_c = np.tile(np.concatenate([-sr, sr, -sc, sc], -1), (1, 2))
    return tuple(jnp.asarray(t, F32) for t in (cos_a, sin_a, cos_c, sin_c))


def _head_mean_matrix():
    m = np.kron(np.eye(LANES // HEAD_DIM), np.full((HEAD_DIM, HEAD_DIM), 1.0 / HEAD_DIM))
    return jnp.asarray(np.concatenate([m, m], 0), BF16)


def _hyena_positions():
    L = SEQ
    t = np.linspace(0.0, 1.0, L)
    bands = np.linspace(1e-4, HY_BANDS - 1, HY_BANDS)
    ang = 2.0 * math.pi * bands[None, :] * np.arange(L)[:, None] / L
    z = np.concatenate([t[:, None], np.cos(ang), -np.sin(ang)], -1)
    zt = np.zeros((LANES, 2 * L))
    zt[:HY_EMB, :L] = z[::-1].T
    zt[:HY_EMB, L:] = z.T
    trow = np.concatenate([t[::-1], t])[None, :]
    return jnp.asarray(zt, F32), jnp.asarray(trow, F32)


def _rope(x, cos, sin_signed, half, lane):
    first = (lane & (2 * half - 1)) < half
    swapped = jnp.where(first, pltpu.roll(x, LANES - half, 1), pltpu.roll(x, half, 1))
    return x * cos + swapped * sin_signed


def _rms(x, gain):
    ms = jnp.mean(x * x, axis=-1, keepdims=True)
    return x * lax.rsqrt(ms + EPS) * gain


IN_TM = 2048
IN_RB = 512
IN_STEPS = IN_TM // IN_RB


def _inproj_kernel(x_ref, g_ref, w_ref, cosa_ref, sina_ref, cosc_ref, sinc_ref, gq_ref, gk_ref, bd_ref,
                   qa_ref, ka_ref, va_ref, hy_ref, qc_ref, kc_ref, vc_ref, h_ref):
    lane = lax.broadcasted_iota(jnp.int32, (IN_RB, LANES), 1)

    def normed(i):
        rows = pl.ds(pl.multiple_of(i * IN_RB, IN_RB), IN_RB)
        return _rms(x_ref[rows, :], g_ref[...]).astype(BF16)

    h_ref[0] = normed(0)

    def head_norm(blocks, gains):
        hi_lo = jnp.concatenate([jnp.concatenate(_split_hi_lo(v * v), axis=1) for v in blocks], axis=0)
        ms = jnp.dot(hi_lo, bd_ref[...], preferred_element_type=F32)
        return [v * lax.rsqrt(ms[i * IN_RB:(i + 1) * IN_RB] + EPS) * gain
                for i, (v, gain) in enumerate(zip(blocks, gains))]

    def sub(i, carry):
        rows = pl.ds(pl.multiple_of(i * IN_RB, IN_RB), IN_RB)
        slot = i % 2
        h = h_ref[slot]
        h_ref[1 - slot] = normed(jnp.minimum(i + 1, IN_STEPS - 1))
        cosa, sina = cosa_ref[rows, :], sina_ref[rows, :]
        cosc, sinc = cosc_ref[rows, :], sinc_ref[rows, :]
        p = jnp.dot(h, w_ref[...], preferred_element_type=F32)
        block = lambda gb: p[:, gb * LANES:(gb + 1) * LANES]
        normed_qk = head_norm([block(gb) for gb in range(15, 19)], [gq_ref[...]] * 3 + [gk_ref[...]])
        for gb in range(PROJ_WIDTH // LANES):
            blk = block(gb)
            if gb < 3:
                cols = slice(gb * LANES, (gb + 1) * LANES)
                qa_ref[rows, cols] = (_rope(blk, cosa, sina, 32, lane) * QSCALE).astype(BF16)
            elif gb < 6:
                cols = slice((gb - 3) * LANES, (gb - 2) * LANES)
                ka_ref[rows, cols] = _rope(blk, cosa, sina, 32, lane).astype(BF16)
            elif gb < 9:
                cols = slice((gb - 6) * LANES, (gb - 5) * LANES)
                va_ref[rows, cols] = blk.astype(BF16)
            elif gb < 15:
                cols = slice((gb - 9) * LANES, (gb - 8) * LANES)
                hy_ref[rows, cols] = blk.astype(BF16)
            elif gb < 18:
                cols = slice((gb - 15) * LANES, (gb - 14) * LANES)
                q = _rope(normed_qk[gb - 15], cosc, sinc, 16, lane)
                qc_ref[rows, cols] = (q * QSCALE).astype(BF16)
            elif gb == 18:
                kc_ref[rows, :] = _rope(normed_qk[3], cosc, sinc, 16, lane).astype(BF16)
            else:
                vc_ref[rows, :] = blk.astype(BF16)
        return carry

    lax.fori_loop(0, IN_STEPS, sub, 0)


def _inproj(x2, g, w, tables, gq, gk, bd):
    n = x2.shape[0]
    tiles_per_seq = SEQ // IN_TM
    row = lambda width: pl.BlockSpec((IN_TM, width), lambda i: (i, 0))
    tab = pl.BlockSpec((IN_TM, LANES), lambda i: (i % tiles_per_seq, 0))
    outs = [A_WIDTH, A_WIDTH, A_WIDTH, 3 * HY_WIDTH, C_WIDTH, C_KV_WIDTH, C_KV_WIDTH]
    return pl.pallas_call(
        _inproj_kernel,
        grid=(n // IN_TM,),
        in_specs=[row(D_MODEL), _resident((1, D_MODEL)), _resident((D_MODEL, PROJ_WIDTH)),
                  tab, tab, tab, tab, _resident((1, LANES)), _resident((1, LANES)), _resident((2 * LANES, LANES))],
        out_specs=[row(wd) for wd in outs],
        out_shape=[jax.ShapeDtypeStruct((n, wd), BF16) for wd in outs],
        scratch_shapes=[pltpu.VMEM((2, IN_RB, D_MODEL), BF16)],
        compiler_params=_params(),
        name="inproj",
    )(x2, g, w, *tables, gq, gk, bd)


DL_BQ = 128
DL_BK = 256
DL_PER_STEP = 2
DL_BLK = 256
DL_NBLK = SEQ // DL_BLK
DL_NPROB = SEQ // DL_BQ


DL_R = 16
DL_SEG = SEQ // DL_R


def _stat_lane(h):
    return HEAD_DIM * (h % 2) + h // 2


def _dilated_constants():
    per = DL_BLK // DL_R
    p = np.zeros((DL_BLK, DL_BLK))
    for i in range(per):
        for r in range(DL_R):
            p[r * per + i, DL_R * i + r] = 1.0
    e = np.zeros((LANES, A_WIDTH))
    for h in range(A_WIDTH // HEAD_DIM):
        e[_stat_lane(h), h * HEAD_DIM:(h + 1) * HEAD_DIM] = 1.0
    return jnp.asarray(p, BF16), jnp.asarray(p.T, BF16), jnp.asarray(np.concatenate([e, e], 0), BF16)


def _band_problems(probs, lane):
    lo = lax.broadcasted_iota(jnp.int32, (1, LANES), 1) < HEAD_DIM
    npair = A_WIDTH // LANES
    scores = []
    for q, k, _, _ in probs:
        nq = q.shape[0]
        ss = []
        for j in range(npair):
            cols = slice(j * LANES, (j + 1) * LANES)
            qj, zero = q[:, cols], jnp.zeros_like(q[:, cols])
            qst = jnp.concatenate([jnp.where(lo, qj, zero), jnp.where(lo, zero, qj)], axis=0)
            s2 = lax.dot_general(qst, k[:, cols], (((1,), (1,)), ((), ())), preferred_element_type=F32)
            ss += [s2[:nq], s2[nq:]]
        scores.append(ss)
    results = []
    for (q, _, v, mask), ss in zip(probs, scores):
        nq, nk = q.shape[0], v.shape[0]
        ones = jnp.ones((nk, LANES), BF16)
        ps = []
        mtile = jnp.zeros((nq, LANES), F32)
        for h, s in enumerate(ss):
            s = s + mask
            m = jnp.max(s, axis=-1, keepdims=True)
            ps.append(jnp.exp2(s - m).astype(BF16))
            mtile = jnp.where(lane == _stat_lane(h), m, mtile)
        outs = []
        ltile = jnp.ones((nq, LANES), F32)
        for j in range(npair):
            vext = jnp.concatenate([v[:, j * LANES:(j + 1) * LANES], ones], axis=1)
            d = jnp.dot(jnp.concatenate(ps[2 * j:2 * j + 2], axis=0), vext, preferred_element_type=F32)
            num0, den0, num1, den1 = d[:nq, :LANES], d[:nq, LANES:], d[nq:, :LANES], d[nq:, LANES:]
            outs.append(jnp.where(lo, num0 * (1.0 / den0), num1 * (1.0 / den1)))
            ltile = jnp.where(lane == j, den0, jnp.where(lane == HEAD_DIM + j, den1, ltile))
        results.append((jnp.concatenate(outs, axis=1), mtile + jnp.log2(ltile)))
    return results


def _split_hi_lo(x):
    hi = x.astype(BF16)
    return hi, (x - hi.astype(F32)).astype(BF16)


def _dilated_kernel(q_ref, k_ref, v_ref, pf_ref, pb_ref, ex_ref, out_ref, xp_ref, ob_ref, l1_ref, st_ref):
    lane = lax.broadcasted_iota(jnp.int32, (DL_BQ, LANES), 1)
    per = DL_BLK // DL_R

    def block_chunks(blk):
        return [pl.ds(pl.multiple_of(r * DL_SEG + blk * per, per), per) for r in range(DL_R)]

    def regroup(blk, carry):
        rows = pl.ds(pl.multiple_of(blk * DL_BLK, DL_BLK), DL_BLK)
        x = jnp.concatenate([q_ref[rows, :], k_ref[rows, :], v_ref[rows, :]], axis=1)
        y = jnp.dot(pf_ref[...], x, preferred_element_type=F32).astype(BF16)
        for r, at in enumerate(block_chunks(blk)):
            for a in range(3):
                xp_ref[a, at, :] = y[r * per:(r + 1) * per, a * A_WIDTH:(a + 1) * A_WIDTH]
        return carry

    lax.fori_loop(0, DL_NBLK, regroup, 0)

    def run(n_problems, per_step, gather, scatter):
        def step(i, carry):
            ids = [i * per_step + u for u in range(per_step)]
            for p, (o, lse) in zip(ids, _band_problems([gather(p) for p in ids], lane)):
                scatter(p, o.astype(BF16), lse)
            return carry

        lax.fori_loop(0, n_problems // per_step, step, 0)

    def band_mask(diff):
        return jnp.where((diff <= N_SIDE) & (diff >= -N_SIDE), 0.0, NEG)

    def put_regrouped(b, rows, o, lse):
        hi, lo = _split_hi_lo(lse)
        ob_ref[b, rows, :] = o
        st_ref[b - 1, rows, :LANES] = hi
        st_ref[b - 1, rows, LANES:] = lo

    row = lax.broadcasted_iota(jnp.int32, (DL_BQ, DL_BK), 0)
    col = lax.broadcasted_iota(jnp.int32, (DL_BQ, DL_BK), 1)

    def gather1(p):
        t0 = pl.multiple_of(p * DL_BQ, DL_BQ)
        ks = pl.multiple_of(jnp.clip(t0 - N_SIDE, 0, SEQ - DL_BK), N_SIDE)
        keys = pl.ds(ks, DL_BK)
        return q_ref[pl.ds(t0, DL_BQ), :], k_ref[keys, :], v_ref[keys, :], band_mask(col - row + (ks - t0))

    def scatter1(p, o, lse):
        rows = pl.ds(pl.multiple_of(p * DL_BQ, DL_BQ), DL_BQ)
        ob_ref[0, rows, :] = o
        l1_ref[rows, :] = lse

    run(DL_NPROB, DL_PER_STEP, gather1, scatter1)

    n4, qrows, krows = 4, DL_BQ // 4, DL_BK // 4
    slab_of = lambda idx, n: lax.shift_right_logical(idx, n.bit_length() - 1)
    ddiff = 4 * ((col & (krows - 1)) - (row & (qrows - 1))) + (slab_of(col, krows) - slab_of(row, qrows))

    def gather4(p):
        c, b = p // n4, p % n4
        q0 = pl.multiple_of(b * qrows, qrows)
        ks = pl.multiple_of(jnp.clip(q0 - N_SIDE // 4, 0, DL_SEG - krows), N_SIDE // 4)
        slabs = [(c + 4 * s) * DL_SEG for s in range(4)]
        take = lambda a, start, n: jnp.concatenate([xp_ref[a, pl.ds(s0 + start, n), :] for s0 in slabs], axis=0)
        return (take(0, q0, qrows), take(1, ks, krows), take(2, ks, krows), band_mask(ddiff + 4 * (ks - q0)))

    def scatter4(p, o, lse):
        c, b = p // n4, p % n4
        for s in range(4):
            rows = pl.ds(pl.multiple_of((c + 4 * s) * DL_SEG + b * qrows, qrows), qrows)
            put_regrouped(1, rows, o[s * qrows:(s + 1) * qrows], lse[s * qrows:(s + 1) * qrows])

    run(DL_NPROB, DL_PER_STEP, gather4, scatter4)

    mask16 = band_mask(lax.broadcasted_iota(jnp.int32, (DL_BQ, DL_SEG), 1)
                       - lax.broadcasted_iota(jnp.int32, (DL_BQ, DL_SEG), 0))

    def gather16(p):
        rows = pl.ds(pl.multiple_of(p * DL_SEG, DL_SEG), DL_SEG)
        return xp_ref[0, rows, :], xp_ref[1, rows, :], xp_ref[2, rows, :], mask16

    def scatter16(p, o, lse):
        put_regrouped(2, pl.ds(pl.multiple_of(p * DL_SEG, DL_SEG), DL_SEG), o, lse)

    run(DL_R, 2 * DL_PER_STEP, gather16, scatter16)

    def merge(blk, carry):
        rows = pl.ds(pl.multiple_of(blk * DL_BLK, DL_BLK), DL_BLK)
        z = jnp.concatenate(
            [jnp.concatenate([ob_ref[1, at, :], st_ref[0, at, :], ob_ref[2, at, :], st_ref[1, at, :]], axis=1)
             for at in block_chunks(blk)], axis=0)
        nat = jnp.dot(pb_ref[...], z, preferred_element_type=F32)
        w0 = A_WIDTH + 2 * LANES
        o4, s4 = nat[:, :A_WIDTH], nat[:, A_WIDTH:A_WIDTH + LANES] + nat[:, A_WIDTH + LANES:w0]
        o16, s16 = nat[:, w0:w0 + A_WIDTH], nat[:, w0 + A_WIDTH:w0 + A_WIDTH + LANES] + nat[:, w0 + A_WIDTH + LANES:]
        o1, s1 = ob_ref[0, rows, :].astype(F32), l1_ref[rows, :]
        mx = jnp.maximum(jnp.maximum(s1, s4), s16)
        es = [jnp.exp2(s - mx) for s in (s1, s4, s16)]
        inv = 1.0 / (es[0] + es[1] + es[2])
        spread = lambda w: jnp.dot(jnp.concatenate(_split_hi_lo(w), axis=1), ex_ref[...], preferred_element_type=F32)
        w1, w4 = spread(es[0] * inv), spread(es[1] * inv)
        out_ref[rows, :] = (w1 * o1 + w4 * o4 + (1.0 - w1 - w4) * o16).astype(BF16)
        return carry

    lax.fori_loop(0, DL_NBLK, merge, 0)


def _dilated(qa, ka, va, batch):
    seq = pl.BlockSpec((None, SEQ, A_WIDTH), lambda b: (b, 0, 0))
    pf, pb, ex = _dilated_constants()
    shaped = lambda a: a.reshape(batch, SEQ, A_WIDTH)
    out = pl.pallas_call(
        _dilated_kernel, grid=(batch,),
        in_specs=[seq, seq, seq, _resident(pf.shape), _resident(pb.shape), _resident(ex.shape)],
        out_specs=seq,
        out_shape=jax.ShapeDtypeStruct((batch, SEQ, A_WIDTH), BF16),
        scratch_shapes=[pltpu.VMEM((3, SEQ, A_WIDTH), BF16), pltpu.VMEM((3, SEQ, A_WIDTH), BF16),
                        pltpu.VMEM((SEQ, LANES), F32), pltpu.VMEM((2, SEQ, 2 * LANES), BF16)],
        compiler_params=_params(), name="dilated",
    )(shaped(qa), shaped(ka), shaped(va), pf, pb, ex)
    return out.reshape(batch * SEQ, A_WIDTH)


GQ_TQ = 512


def _gqa_kernel(q_ref, k_ref, v_ref, o_ref, ksw_ref, vm_ref):
    lane = lax.broadcasted_iota(jnp.int32, (SEQ, LANES), 1)
    lo = lane < HEAD_DIM
    k = k_ref[...]
    v = v_ref[...]
    ksw_ref[...] = pltpu.roll(k.astype(F32), HEAD_DIM, 1).astype(BF16)
    vsw = pltpu.roll(v.astype(F32), HEAD_DIM, 1).astype(BF16)
    zero = jnp.zeros_like(v)
    vm_ref[0, :, :LANES] = jnp.where(lo, v, zero)
    vm_ref[1, :, :LANES] = jnp.where(lo, zero, vsw)
    vm_ref[2, :, :LANES] = jnp.where(lo, vsw, zero)
    vm_ref[3, :, :LANES] = jnp.where(lo, zero, v)
    for idx in range(4):
        vm_ref[idx, :, LANES:] = jnp.ones((SEQ, LANES), BF16)
    lo_q = lax.broadcasted_iota(jnp.int32, (1, LANES), 1) < HEAD_DIM

    nh = 2 * (C_WIDTH // LANES)

    def blk(i, carry):
        rows = pl.ds(pl.multiple_of(i * GQ_TQ, GQ_TQ), GQ_TQ)

        def scores(h):
            j, half = divmod(h, 2)
            qj = q_ref[rows, j * LANES:(j + 1) * LANES]
            sel = lo_q if half == 0 else jnp.logical_not(lo_q)
            qm = jnp.where(sel, qj, jnp.zeros_like(qj))
            kk = k_ref[...] if h // 3 == half else ksw_ref[...]
            return lax.dot_general(qm, kk, (((1,), (1,)), ((), ())), preferred_element_type=F32)

        def softmax(s):
            return jnp.exp2(s - jnp.max(s, axis=-1, keepdims=True)).astype(BF16)

        def values(h, pb):
            d = jnp.dot(pb, vm_ref[2 * (h // 3) + h % 2], preferred_element_type=F32)
            return d[:, :LANES] * (1.0 / d[:, LANES:])

        pending = {0: scores(0), 1: scores(1)}
        outs = []
        for h in range(nh):
            pb = softmax(pending.pop(h))
            if h + 2 < nh:
                pending[h + 2] = scores(h + 2)
            outs.append(values(h, pb))
        for j in range(nh // 2):
            o_ref[rows, j * LANES:(j + 1) * LANES] = (outs[2 * j] + outs[2 * j + 1]).astype(BF16)
        return carry

    lax.fori_loop(0, SEQ // GQ_TQ, blk, 0)


def _gqa(qc, kc, vc, batch):
    spec = lambda width: pl.BlockSpec((None, SEQ, width), lambda b: (b, 0, 0))
    out = pl.pallas_call(
        _gqa_kernel, grid=(batch,),
        in_specs=[spec(C_WIDTH), spec(C_KV_WIDTH), spec(C_KV_WIDTH)],
        out_specs=spec(C_WIDTH),
        out_shape=jax.ShapeDtypeStruct((batch, SEQ, C_WIDTH), BF16),
        scratch_shapes=[pltpu.VMEM((SEQ, LANES), BF16), pltpu.VMEM((4, SEQ, 2 * LANES), BF16)],
        compiler_params=_params(), name="gqa",
    )(qc.reshape(batch, SEQ, C_WIDTH), kc.reshape(batch, SEQ, C_KV_WIDTH), vc.reshape(batch, SEQ, C_KV_WIDTH))
    return out.reshape(batch * SEQ, C_WIDTH)


def _filter_kernel(z_ref, trow_ref, w1t_ref, b1_ref, f0_ref, w2t_ref, b2_ref, f1_ref, w3t_ref, dec_ref, out_ref):
    L = SEQ
    dot = functools.partial(jnp.dot, precision=HIGHEST, preferred_element_type=F32)
    h = jnp.sin(f0_ref[...] * (dot(w1t_ref[...], z_ref[...]) + b1_ref[...]))
    h = jnp.sin(f1_ref[...] * (dot(w2t_ref[...], h) + b2_ref[...]))
    w3t = w3t_ref[...]
    dec = dec_ref[...]
    trow = trow_ref[...]
    hf = dot(w3t[:HY_WIDTH], h[:, L:]) * jnp.exp(-trow[:, L:] * dec[:HY_WIDTH])
    hb = dot(w3t[HY_WIDTH:], h[:, :L]) * jnp.exp(-trow[:, :L] * dec[HY_WIDTH:])
    col = lax.broadcasted_iota(jnp.int32, hb.shape, 1)
    hb = jnp.where(col == 0, 0.0, pltpu.roll(hb, 1, 1))
    norm = (jnp.sum(jnp.abs(hf), axis=-1, keepdims=True) + jnp.sum(jnp.abs(hb), axis=-1, keepdims=True))
    out_ref[:, :L] = hb / norm
    out_ref[:, L:] = hf / norm


def _filters(zt, trow, w1, b1, freq, w2, b2, w3, decay):
    colv = lambda a: a.reshape(-1, 1).astype(F32)
    w1t = jnp.zeros((HY_HIDDEN, LANES), F32).at[:, :HY_EMB].set(w1.T)
    order = pl.BlockSpec((2 * HY_WIDTH, HY_HIDDEN), lambda o: (o, 0))
    return pl.pallas_call(
        _filter_kernel, grid=(2,),
        in_specs=[_resident(zt.shape), _resident(trow.shape), _resident(w1t.shape), _resident((HY_HIDDEN, 1)),
                  _resident((HY_HIDDEN, 1)), _resident((HY_HIDDEN, HY_HIDDEN)), _resident((HY_HIDDEN, 1)),
                  _resident((HY_HIDDEN, 1)), order, pl.BlockSpec((2 * HY_WIDTH, 1), lambda o: (o, 0))],
        out_specs=pl.BlockSpec((HY_WIDTH, 2 * SEQ), lambda o: (o, 0)),
        out_shape=jax.ShapeDtypeStruct((2 * HY_WIDTH, 2 * SEQ), F32),
        compiler_params=_params(), name="filters",
    )(zt, trow, w1t, colv(b1), colv(freq[0]), w2.T, colv(b2), colv(freq[1]), w3.T, colv(decay))


HY_TB = 256
HY_NB = SEQ // HY_TB


HY_CH = 8


def _hyena_kernel(cw_ref, cb_ref, d_ref, pv_ref, px1_ref, px2_ref, g0_ref, g1_ref, o_ref, gsh_ref, ust_ref, acc_ref):
    c0 = pl.program_id(0) * HY_CH
    nb = pv_ref.shape[1]
    col = lax.broadcasted_iota(jnp.int32, (nb, SEQ), 1)
    chans = range(HY_CH)

    def build_strip(cc, order):
        g = jnp.broadcast_to((g0_ref, g1_ref)[order][cc:cc + 1, :], (16, 2 * SEQ))
        base = pltpu.bitcast(pltpu.roll(g, 0, 1, stride=1, stride_axis=0).astype(BF16), jnp.uint32)
        for a in range(LANES // 16):
            rows = base if a == 0 else pltpu.roll(base, 16 * a, 1)
            gsh_ref[2 * cc + order, 16 * a:16 * (a + 1), :] = pltpu.bitcast(rows, BF16)

    def dwconv(p_ref, cc, ch):
        p = p_ref[cc].astype(F32)
        prev = jnp.where(col == 0, 0.0, pltpu.roll(p, 1, 1))
        nxt = jnp.where(col == SEQ - 1, 0.0, pltpu.roll(p, SEQ - 1, 1))
        return prev * cw_ref[0, ch] + p * cw_ref[1, ch] + nxt * cw_ref[2, ch] + cb_ref[ch]

    def longconv(cc, order, u):
        gsh = gsh_ref.at[2 * cc + order]
        ust, acc = ust_ref.at[cc], acc_ref.at[cc]
        ub = u.astype(BF16)
        for tb in range(HY_NB):
            ust[tb * nb:(tb + 1) * nb, :] = ub[:, tb * HY_TB:(tb + 1) * HY_TB]
        acc[...] = jnp.zeros(acc.shape, F32)
        for dl in range(-(HY_NB - 1), HY_NB):
            x0 = HY_TB * (dl + HY_NB)
            rhs = jnp.concatenate([gsh[:, x0:x0 + HY_TB], gsh[:, x0 - LANES:x0 - LANES + HY_TB]], axis=0)
            b0, b1 = max(0, -dl), min(HY_NB, HY_NB - dl)
            acc[(b0 + dl) * nb:(b1 + dl) * nb, :] += jnp.dot(ust[b0 * nb:b1 * nb, :], rhs, preferred_element_type=F32)
        y = jnp.concatenate([acc[tb * nb:(tb + 1) * nb, :] for tb in range(HY_NB)], axis=1)
        return y + d_ref[order, c0 + cc] * u

    y1, y2 = [], []
    for cc in chans:
        build_strip(cc, 0)
        y1.append(longconv(cc, 0, dwconv(pv_ref, cc, c0 + cc)))
    for cc in chans:
        build_strip(cc, 1)
        y2.append(longconv(cc, 1, dwconv(px1_ref, cc, HY_WIDTH + c0 + cc) * y1[cc]))
    for cc in chans:
        o_ref[cc] = (dwconv(px2_ref, cc, 2 * HY_WIDTH + c0 + cc) * y2[cc]).astype(BF16)


def _hyena(pt, gt, conv_w, conv_b, dbias):
    nb = pt.shape[1]
    smem = pl.BlockSpec(memory_space=pltpu.SMEM)
    per_group = HY_WIDTH // HY_CH
    chan = lambda group: pl.BlockSpec((HY_CH, nb, SEQ), lambda c: (c + group * per_group, 0, 0))
    filt = lambda group: pl.BlockSpec((HY_CH, 2 * SEQ), lambda c: (c + group * per_group, 0))
    return pl.pallas_call(
        _hyena_kernel, grid=(per_group,),
        in_specs=[smem, smem, smem, chan(0), chan(1), chan(2), filt(0), filt(1)],
        out_specs=pl.BlockSpec((HY_CH, nb, SEQ), lambda c: (c, 0, 0)),
        out_shape=jax.ShapeDtypeStruct((HY_WIDTH, nb, SEQ), BF16),
        scratch_shapes=[pltpu.VMEM((2 * HY_CH, LANES, 2 * SEQ), BF16), pltpu.VMEM((HY_CH, HY_NB * nb, HY_TB), BF16),
                        pltpu.VMEM((HY_CH, HY_NB * nb, HY_TB), F32)],
        compiler_params=_params(), name="hyena",
    )(conv_w, conv_b, dbias, pt, pt, pt, gt, gt)


OP_TM = 2048
OP_RB = 512


def _outproj_kernel(oa_ref, hy_ref, oc_ref, w_ref, x_ref, g_ref, out_ref):
    def sub(i, carry):
        blocks = [pl.ds(pl.multiple_of((2 * i + u) * OP_RB, OP_RB), OP_RB) for u in range(2)]
        mixes = [jnp.dot(jnp.concatenate([oa_ref[rows, :], hy_ref[rows, :], oc_ref[rows, :]], axis=1), w_ref[...],
                         preferred_element_type=F32) for rows in blocks]
        for rows, mix in zip(blocks, mixes):
            out_ref[rows, :] = x_ref[rows, :] + _rms(mix, g_ref[...])
        return carry

    lax.fori_loop(0, OP_TM // (2 * OP_RB), sub, 0)


def _outproj(oa, hyo, oc, w, x2, g):
    n = x2.shape[0]
    row = lambda width: pl.BlockSpec((OP_TM, width), lambda i: (i, 0))
    return pl.pallas_call(
        _outproj_kernel, grid=(n // OP_TM,),
        in_specs=[row(A_WIDTH), row(HY_WIDTH), row(C_WIDTH),
                  _resident((D_MODEL, D_MODEL)), row(D_MODEL), _resident((1, D_MODEL))],
        out_specs=row(D_MODEL),
        out_shape=jax.ShapeDtypeStruct((n, D_MODEL), F32),
        compiler_params=_params(), name="outproj",
    )(oa, hyo, oc, w, x2, g)


FF_TM = 1024
FF_RB = 256
FF_HALO = 8


def _ffn_kernel(x_ref, xp_ref, xn_ref, gpre_ref, wg_ref, wu_ref, cw_ref, cb_ref, wd_ref, gpost_ref, out_ref, xs_ref):
    i = pl.program_id(0)
    tiles_per_seq = SEQ // FF_TM
    first = (i % tiles_per_seq) == 0
    last = (i % tiles_per_seq) == tiles_per_seq - 1
    xs_ref[:FF_HALO, :] = jnp.where(first, 0.0, xp_ref[...])
    xs_ref[FF_HALO:FF_HALO + FF_TM, :] = x_ref[...]
    xs_ref[FF_HALO + FF_TM:, :] = jnp.where(last, 0.0, xn_ref[...])

    acts, xs = [], []
    for s in range(FF_TM // FF_RB):
        xh = xs_ref[s * FF_RB:s * FF_RB + FF_RB + 2 * FF_HALO, :]
        x = xh[FF_HALO:FF_HALO + FF_RB]
        gate = jnp.dot(_rms(xh, gpre_ref[...]).astype(BF16), wg_ref[...], preferred_element_type=F32)
        up = jnp.dot(_rms(x, gpre_ref[...]).astype(BF16), wu_ref[...], preferred_element_type=F32)
        gc = (gate[FF_HALO - 1:FF_HALO - 1 + FF_RB] * cw_ref[0:1, :]
              + gate[FF_HALO:FF_HALO + FF_RB] * cw_ref[1:2, :]
              + gate[FF_HALO + 1:FF_HALO + 1 + FF_RB] * cw_ref[2:3, :] + cb_ref[...])
        acts.append((jax.nn.gelu(gc, approximate=True) * up).astype(BF16))
        xs.append(x)
    for s, (act, x) in enumerate(zip(acts, xs)):
        f = jnp.dot(act, wd_ref[...], preferred_element_type=F32)
        out_ref[s * FF_RB:(s + 1) * FF_RB, :] = x + _rms(f, gpost_ref[...])


def _ffn(x2, gpre, wg, wu, cw, cb, wd, gpost):
    n = x2.shape[0]
    per = FF_TM // FF_HALO
    last_blk = n // FF_HALO - 1
    return pl.pallas_call(
        _ffn_kernel, grid=(n // FF_TM,),
        in_specs=[pl.BlockSpec((FF_TM, D_MODEL), lambda i: (i, 0)),
                  pl.BlockSpec((FF_HALO, D_MODEL), lambda i: (jnp.maximum(i * per - 1, 0), 0)),
                  pl.BlockSpec((FF_HALO, D_MODEL), lambda i: (jnp.minimum((i + 1) * per, last_blk), 0)),
                  _resident((1, D_MODEL)), _resident((D_MODEL, D_FF)), _resident((D_MODEL, D_FF)),
                  _resident((3, D_FF)), _resident((1, D_FF)), _resident((D_FF, D_MODEL)), _resident((1, D_MODEL))],
        out_specs=pl.BlockSpec((FF_TM, D_MODEL), lambda i: (i, 0)),
        out_shape=jax.ShapeDtypeStruct((n, D_MODEL), F32),
        scratch_shapes=[pltpu.VMEM((FF_TM + 2 * FF_HALO, D_MODEL), F32)],
        compiler_params=_params(), name="ffn",
    )(x2, x2, x2, gpre, wg, wu, cw, cb, wd, gpost)


def kernel(x, g_mix_pre, g_mix_post, g_ffn_pre, g_ffn_post, w_in, w_out, g_q, g_k, hy_conv_w, hy_conv_b, hy_w1, hy_b1, hy_freq, hy_w2, hy_b2, hy_w3, hy_decay, hy_d, ffn_w_gate, ffn_w_up, ffn_conv_w, ffn_conv_b, ffn_w_down):
    batch = x.shape[0]
    n = batch * SEQ
    tables = _rope_tables()
    bd = _head_mean_matrix()
    zt, trow = _hyena_positions()
    rowv = lambda a: a.reshape(1, -1).astype(F32)
    x2 = x.reshape(n, D_MODEL)
    for i in range(DEPTH):
        qa, ka, va, hy, qc, kc, vc = _inproj(
            x2, rowv(g_mix_pre[i]), w_in[i].astype(BF16), tables,
            rowv(jnp.tile(g_q[i], 2)), rowv(jnp.tile(g_k[i], 2)), bd)
        oa = _dilated(qa, ka, va, batch)
        oc = _gqa(qc, kc, vc, batch)
        gt = _filters(zt, trow, hy_w1[i], hy_b1[i], hy_freq[i], hy_w2[i], hy_b2[i], hy_w3[i], hy_decay[i])
        pt = jnp.transpose(hy.reshape(batch, SEQ, 3 * HY_WIDTH), (2, 0, 1))
        ot = _hyena(pt, gt, hy_conv_w[i].astype(F32), hy_conv_b[i].astype(F32), hy_d[i].astype(F32))
        hyo = jnp.transpose(ot, (1, 2, 0)).reshape(n, HY_WIDTH)
        x2 = _outproj(oa, hyo, oc, w_out[i].astype(BF16), x2, rowv(g_mix_post[i]))
        x2 = _ffn(x2, rowv(g_ffn_pre[i]), ffn_w_gate[i].astype(BF16), ffn_w_up[i].astype(BF16),
                  ffn_conv_w[i].astype(F32), rowv(ffn_conv_b[i]), ffn_w_down[i].astype(BF16),
                  rowv(g_ffn_post[i]))
    return x2.reshape(batch, SEQ, D_MODEL)
```

```python
import functools
import math

import numpy as np
import jax
import jax.numpy as jnp
from jax import lax
from jax.experimental import pallas as pl
from jax.experimental.pallas import tpu as pltpu

F32 = jnp.float32
BF16 = jnp.bfloat16

D_MODEL = 1024
SEQ = 2048
DEPTH = 2
HEAD_DIM = 64
A_WIDTH = 384
HY_WIDTH = 256
C_WIDTH = 384
C_KV_WIDTH = 128
PROJ_WIDTH = 2560
GRID_W = 64
ROPE_THETA = 10000.0
HY_BANDS = 16
HY_EMB = 33
HY_HIDDEN = 64
D_FF = 2816
EPS = 1e-6
N_SIDE = 64
QSCALE = HEAD_DIM ** -0.5 * math.log2(math.e)
NEG = -1e30

LANES = 128
VMEM_LIMIT = 56 * 1024 * 1024

HIGHEST = lax.Precision.HIGHEST


def _params(n_grid_dims=1):
    return pltpu.CompilerParams(
        dimension_semantics=("arbitrary",) * n_grid_dims, vmem_limit_bytes=VMEM_LIMIT)


def _resident(shape):
    nd = len(shape)
    return pl.BlockSpec(shape, lambda *_: (0,) * nd, pipeline_mode=pl.Buffered(1))


def _rope_tables():
    def angles(pos, dim):
        freqs = ROPE_THETA ** (-np.arange(0, dim, 2, dtype=np.float64) / dim)
        ang = pos.astype(np.float64)[:, None] * freqs[None, :]
        return np.cos(ang), np.sin(ang)

    pos = np.arange(SEQ)
    c, s = angles(pos, HEAD_DIM)
    cos_a = np.tile(np.concatenate([c, c], -1), (1, 2))
    sin_a = np.tile(np.concatenate([-s, s], -1), (1, 2))
    cr, sr = angles(pos // GRID_W, HEAD_DIM // 2)
    cc, sc = angles(pos % GRID_W, HEAD_DIM // 2)
    cos_c = np.tile(np.concatenate([cr, cr, cc, cc], -1), (1, 2))
    sin_c = np.tile(np.concatenate([-sr, sr, -sc, sc], -1), (1, 2))
    return tuple(jnp.asarray(t, F32) for t in (cos_a, sin_a, cos_c, sin_c))


def _head_mean_matrix():
    m = np.kron(np.eye(LANES // HEAD_DIM), np.full((HEAD_DIM, HEAD_DIM), 1.0 / HEAD_DIM))
    return jnp.asarray(np.concatenate([m, m], 0), BF16)


def _hyena_positions():
    L = SEQ
    t = np.linspace(0.0, 1.0, L)
    bands = np.linspace(1e-4, HY_BANDS - 1, HY_BANDS)
    ang = 2.0 * math.pi * bands[None, :] * np.arange(L)[:, None] / L
    z = np.concatenate([t[:, None], np.cos(ang), -np.sin(ang)], -1)
    zt = np.zeros((LANES, 2 * L))
    zt[:HY_EMB, :L] = z[::-1].T
    zt[:HY_EMB, L:] = z.T
    trow = np.concatenate([t[::-1], t])[None, :]
    return jnp.asarray(zt, F32), jnp.asarray(trow, F32)


def _rope(x, cos, sin_signed, half, lane):
    first = (lane & (2 * half - 1)) < half
    swapped = jnp.where(first, pltpu.roll(x, LANES - half, 1), pltpu.roll(x, half, 1))
    return x * cos + swapped * sin_signed


def _rms(x, gain):
    ms = jnp.mean(x * x, axis=-1, keepdims=True)
    return x * lax.rsqrt(ms + EPS) * gain


IN_TM = 2048
IN_RB = 512
IN_STEPS = IN_TM // IN_RB


def _inproj_kernel(x_ref, g_ref, w_ref, cosa_ref, sina_ref, cosc_ref, sinc_ref, gq_ref, gk_ref, bd_ref,
                   qa_ref, ka_ref, va_ref, hy_ref, qc_ref, kc_ref, vc_ref, h_ref):
    lane = lax.broadcasted_iota(jnp.int32, (IN_RB, LANES), 1)

    def normed(i):
        rows = pl.ds(pl.multiple_of(i * IN_RB, IN_RB), IN_RB)
        return _rms(x_ref[rows, :], g_ref[...]).astype(BF16)

    h_ref[0] = normed(0)

    def head_norm(blocks, gains):
        hi_lo = jnp.concatenate([jnp.concatenate(_split_hi_lo(v * v), axis=1) for v in blocks], axis=0)
        ms = jnp.dot(hi_lo, bd_ref[...], preferred_element_type=F32)
        return [v * lax.rsqrt(ms[i * IN_RB:(i + 1) * IN_RB] + EPS) * gain
                for i, (v, gain) in enumerate(zip(blocks, gains))]

    def sub(i, carry):
        rows = pl.ds(pl.multiple_of(i * IN_RB, IN_RB), IN_RB)
        slot = i % 2
        h = h_ref[slot]
        h_ref[1 - slot] = normed(jnp.minimum(i + 1, IN_STEPS - 1))
        cosa, sina = cosa_ref[rows, :], sina_ref[rows, :]
        cosc, sinc = cosc_ref[rows, :], sinc_ref[rows, :]
        p = jnp.dot(h, w_ref[...], preferred_element_type=F32)
        block = lambda gb: p[:, gb * LANES:(gb + 1) * LANES]
        normed_qk = head_norm([block(gb) for gb in range(15, 19)], [gq_ref[...]] * 3 + [gk_ref[...]])
        for gb in range(PROJ_WIDTH // LANES):
            blk = block(gb)
            if gb < 3:
                cols = slice(gb * LANES, (gb + 1) * LANES)
                qa_ref[rows, cols] = (_rope(blk, cosa, sina, 32, lane) * QSCALE).astype(BF16)
            elif gb < 6:
                cols = slice((gb - 3) * LANES, (gb - 2) * LANES)
                ka_ref[rows, cols] = _rope(blk, cosa, sina, 32, lane).astype(BF16)
            elif gb < 9:
                cols = slice((gb - 6) * LANES, (gb - 5) * LANES)
                va_ref[rows, cols] = blk.astype(BF16)
            elif gb < 15:
                cols = slice((gb - 9) * LANES, (gb - 8) * LANES)
                hy_ref[rows, cols] = blk.astype(BF16)
            elif gb < 18:
                cols = slice((gb - 15) * LANES, (gb - 14) * LANES)
                q = _rope(normed_qk[gb - 15], cosc, sinc, 16, lane)
                qc_ref[rows, cols] = (q * QSCALE).astype(BF16)
            elif gb == 18:
                kc_ref[rows, :] = _rope(normed_qk[3], cosc, sinc, 16, lane).astype(BF16)
            else:
                vc_ref[rows, :] = blk.astype(BF16)
        return carry

    lax.fori_loop(0, IN_STEPS, sub, 0)


def _inproj(x2, g, w, tables, gq, gk, bd):
    n = x2.shape[0]
    tiles_per_seq = SEQ // IN_TM
    row = lambda width: pl.BlockSpec((IN_TM, width), lambda i: (i, 0))
    tab = pl.BlockSpec((IN_TM, LANES), lambda i: (i % tiles_per_seq, 0))
    outs = [A_WIDTH, A_WIDTH, A_WIDTH, 3 * HY_WIDTH, C_WIDTH, C_KV_WIDTH, C_KV_WIDTH]
    return pl.pallas_call(
        _inproj_kernel,
        grid=(n // IN_TM,),
        in_specs=[row(D_MODEL), _resident((1, D_MODEL)), _resident((D_MODEL, PROJ_WIDTH)),
                  tab, tab, tab, tab, _resident((1, LANES)), _resident((1, LANES)), _resident((2 * LANES, LANES))],
        out_specs=[row(wd) for wd in outs],
        out_shape=[jax.ShapeDtypeStruct((n, wd), BF16) for wd in outs],
        scratch_shapes=[pltpu.VMEM((2, IN_RB, D_MODEL), BF16)],
        compiler_params=_params(),
        name="inproj",
    )(x2, g, w, *tables, gq, gk, bd)


DL_BQ = 128
DL_BK = 256
DL_PER_STEP = 2
DL_BLK = 256
DL_NBLK = SEQ // DL_BLK
DL_NPROB = SEQ // DL_BQ


DL_R = 16
DL_SEG = SEQ // DL_R


def _stat_lane(h):
    return HEAD_DIM * (h % 2) + h // 2


def _dilated_constants():
    per = DL_BLK // DL_R
    p = np.zeros((DL_BLK, DL_BLK))
    for i in range(per):
        for r in range(DL_R):
            p[r * per + i, DL_R * i + r] = 1.0
    e = np.zeros((LANES, A_WIDTH))
    for h in range(A_WIDTH // HEAD_DIM):
        e[_stat_lane(h), h * HEAD_DIM:(h + 1) * HEAD_DIM] = 1.0
    return jnp.asarray(p, BF16), jnp.asarray(p.T, BF16), jnp.asarray(np.concatenate([e, e], 0), BF16)


def _band_problems(probs, lane):
    lo = lax.broadcasted_iota(jnp.int32, (1, LANES), 1) < HEAD_DIM
    npair = A_WIDTH // LANES
    scores = []
    for q, k, _, _ in probs:
        nq = q.shape[0]
        ss = []
        for j in range(npair):
            cols = slice(j * LANES, (j + 1) * LANES)
            qj, zero = q[:, cols], jnp.zeros_like(q[:, cols])
            qst = jnp.concatenate([jnp.where(lo, qj, zero), jnp.where(lo, zero, qj)], axis=0)
            s2 = lax.dot_general(qst, k[:, cols], (((1,), (1,)), ((), ())), preferred_element_type=F32)
            ss += [s2[:nq], s2[nq:]]
        scores.append(ss)
    results = []
    for (q, _, v, mask), ss in zip(probs, scores):
        nq, nk = q.shape[0], v.shape[0]
        ones = jnp.ones((nk, LANES), BF16)
        ps = []
        mtile = jnp.zeros((nq, LANES), F32)
        for h, s in enumerate(ss):
            s = s + mask
            m = jnp.max(s, axis=-1, keepdims=True)
            ps.append(jnp.exp2(s - m).astype(BF16))
            mtile = jnp.where(lane == _stat_lane(h), m, mtile)
        outs = []
        ltile = jnp.ones((nq, LANES), F32)
        for j in range(npair):
            vext = jnp.concatenate([v[:, j * LANES:(j + 1) * LANES], ones], axis=1)
            d = jnp.dot(jnp.concatenate(ps[2 * j:2 * j + 2], axis=0), vext, preferred_element_type=F32)
            num0, den0, num1, den1 = d[:nq, :LANES], d[:nq, LANES:], d[nq:, :LANES], d[nq:, LANES:]
            outs.append(jnp.where(lo, num0 * (1.0 / den0), num1 * (1.0 / den1)))
            ltile = jnp.where(lane == j, den0, jnp.where(lane == HEAD_DIM + j, den1, ltile))
        results.append((jnp.concatenate(outs, axis=1), mtile + jnp.log2(ltile)))
    return results


def _split_hi_lo(x):
    hi = x.astype(BF16)
    return hi, (x - hi.astype(F32)).astype(BF16)


def _dilated_kernel(q_ref, k_ref, v_ref, pf_ref, pb_ref, ex_ref, out_ref, xp_ref, ob_ref, l1_ref, st_ref):
    lane = lax.broadcasted_iota(jnp.int32, (DL_BQ, LANES), 1)
    per = DL_BLK // DL_R

    def block_chunks(blk):
        return [pl.ds(pl.multiple_of(r * DL_SEG + blk * per, per), per) for r in range(DL_R)]

    def regroup(blk, carry):
        rows = pl.ds(pl.multiple_of(blk * DL_BLK, DL_BLK), DL_BLK)
        x = jnp.concatenate([q_ref[rows, :], k_ref[rows, :], v_ref[rows, :]], axis=1)
        y = jnp.dot(pf_ref[...], x, preferred_element_type=F32).astype(BF16)
        for r, at in enumerate(block_chunks(blk)):
            for a in range(3):
                xp_ref[a, at, :] = y[r * per:(r + 1) * per, a * A_WIDTH:(a + 1) * A_WIDTH]
        return carry

    lax.fori_loop(0, DL_NBLK, regroup, 0)

    def run(n_problems, per_step, gather, scatter):
        def step(i, carry):
            ids = [i * per_step + u for u in range(per_step)]
            for p, (o, lse) in zip(ids, _band_problems([gather(p) for p in ids], lane)):
                scatter(p, o.astype(BF16), lse)
            return carry

        lax.fori_loop(0, n_problems // per_step, step, 0)

    def band_mask(diff):
        return jnp.where((diff <= N_SIDE) & (diff >= -N_SIDE), 0.0, NEG)

    def put_regrouped(b, rows, o, lse):
        hi, lo = _split_hi_lo(lse)
        ob_ref[b, rows, :] = o
        st_ref[b - 1, rows, :LANES] = hi
        st_ref[b - 1, rows, LANES:] = lo

    row = lax.broadcasted_iota(jnp.int32, (DL_BQ, DL_BK), 0)
    col = lax.broadcasted_iota(jnp.int32, (DL_BQ, DL_BK), 1)

    def gather1(p):
        t0 = pl.multiple_of(p * DL_BQ, DL_BQ)
        ks = pl.multiple_of(jnp.clip(t0 - N_SIDE, 0, SEQ - DL_BK), N_SIDE)
        keys = pl.ds(ks, DL_BK)
        return q_ref[pl.ds(t0, DL_BQ), :], k_ref[keys, :], v_ref[keys, :], band_mask(col - row + (ks - t0))

    def scatter1(p, o, lse):
        rows = pl.ds(pl.multiple_of(p * DL_BQ, DL_BQ), DL_BQ)
        ob_ref[0, rows, :] = o
        l1_ref[rows, :] = lse

    run(DL_NPROB, DL_PER_STEP, gather1, scatter1)

    n4, qrows, krows = 4, DL_BQ // 4, DL_BK // 4
    slab_of = lambda idx, n: lax.shift_right_logical(idx, n.bit_length() - 1)
    ddiff = 4 * ((col & (krows - 1)) - (row & (qrows - 1))) + (slab_of(col, krows) - slab_of(row, qrows))

    def gather4(p):
        c, b = p // n4, p % n4
        q0 = pl.multiple_of(b * qrows, qrows)
        ks = pl.multiple_of(jnp.clip(q0 - N_SIDE // 4, 0, DL_SEG - krows), N_SIDE // 4)
        slabs = [(c + 4 * s) * DL_SEG for s in range(4)]
        take = lambda a, start, n: jnp.concatenate([xp_ref[a, pl.ds(s0 + start, n), :] for s0 in slabs], axis=0)
        return (take(0, q0, qrows), take(1, ks, krows), take(2, ks, krows), band_mask(ddiff + 4 * (ks - q0)))

    def scatter4(p, o, lse):
        c, b = p // n4, p % n4
        for s in range(4):
            rows = pl.ds(pl.multiple_of((c + 4 * s) * DL_SEG + b * qrows, qrows), qrows)
            put_regrouped(1, rows, o[s * qrows:(s + 1) * qrows], lse[s * qrows:(s + 1) * qrows])

    run(DL_NPROB, DL_PER_STEP, gather4, scatter4)

    mask16 = band_mask(lax.broadcasted_iota(jnp.int32, (DL_BQ, DL_SEG), 1)
                       - lax.broadcasted_iota(jnp.int32, (DL_BQ, DL_SEG), 0))

    def gather16(p):
        rows = pl.ds(pl.multiple_of(p * DL_SEG, DL_SEG), DL_SEG)
        return xp_ref[0, rows, :], xp_ref[1, rows, :], xp_ref[2, rows, :], mask16

    def scatter16(p, o, lse):
        put_regrouped(2, pl.ds(pl.multiple_of(p * DL_SEG, DL_SEG), DL_SEG), o, lse)

    run(DL_R, 2 * DL_PER_STEP, gather16, scatter16)

    def merge(blk, carry):
        rows = pl.ds(pl.multiple_of(blk * DL_BLK, DL_BLK), DL_BLK)
        z = jnp.concatenate(
            [jnp.concatenate([ob_ref[1, at, :], st_ref[0, at, :], ob_ref[2, at, :], st_ref[1, at, :]], axis=1)
             for at in block_chunks(blk)], axis=0)
        nat = jnp.dot(pb_ref[...], z, preferred_element_type=F32)
        w0 = A_WIDTH + 2 * LANES
        o4, s4 = nat[:, :A_WIDTH], nat[:, A_WIDTH:A_WIDTH + LANES] + nat[:, A_WIDTH + LANES:w0]
        o16, s16 = nat[:, w0:w0 + A_WIDTH], nat[:, w0 + A_WIDTH:w0 + A_WIDTH + LANES] + nat[:, w0 + A_WIDTH + LANES:]
        o1, s1 = ob_ref[0, rows, :].astype(F32), l1_ref[rows, :]
        mx = jnp.maximum(jnp.maximum(s1, s4), s16)
        es = [jnp.exp2(s - mx) for s in (s1, s4, s16)]
        inv = 1.0 / (es[0] + es[1] + es[2])
        spread = lambda w: jnp.dot(jnp.concatenate(_split_hi_lo(w), axis=1), ex_ref[...], preferred_element_type=F32)
        w1, w4 = spread(es[0] * inv), spread(es[1] * inv)
        out_ref[rows, :] = (w1 * o1 + w4 * o4 + (1.0 - w1 - w4) * o16).astype(BF16)
        return carry

    lax.fori_loop(0, DL_NBLK, merge, 0)


def _dilated(qa, ka, va, batch):
    seq = pl.BlockSpec((None, SEQ, A_WIDTH), lambda b: (b, 0, 0))
    pf, pb, ex = _dilated_constants()
    shaped = lambda a: a.reshape(batch, SEQ, A_WIDTH)
    out = pl.pallas_call(
        _dilated_kernel, grid=(batch,),
        in_specs=[seq, seq, seq, _resident(pf.shape), _resident(pb.shape), _resident(ex.shape)],
        out_specs=seq,
        out_shape=jax.ShapeDtypeStruct((batch, SEQ, A_WIDTH), BF16),
        scratch_shapes=[pltpu.VMEM((3, SEQ, A_WIDTH), BF16), pltpu.VMEM((3, SEQ, A_WIDTH), BF16),
                        pltpu.VMEM((SEQ, LANES), F32), pltpu.VMEM((2, SEQ, 2 * LANES), BF16)],
        compiler_params=_params(), name="dilated",
    )(shaped(qa), shaped(ka), shaped(va), pf, pb, ex)
    return out.reshape(batch * SEQ, A_WIDTH)


GQ_TQ = 1024


def _gqa_kernel(q_ref, k_ref, v_ref, o_ref, ksw_ref, vm_ref):
    lane = lax.broadcasted_iota(jnp.int32, (SEQ, LANES), 1)
    lo = lane < HEAD_DIM
    k = k_ref[...]
    v = v_ref[...]
    ksw_ref[...] = pltpu.roll(k.astype(F32), HEAD_DIM, 1).astype(BF16)
    vsw = pltpu.roll(v.astype(F32), HEAD_DIM, 1).astype(BF16)
    zero = jnp.zeros_like(v)
    vm_ref[0, :, :LANES] = jnp.where(lo, v, zero)
    vm_ref[1, :, :LANES] = jnp.where(lo, zero, vsw)
    vm_ref[2, :, :LANES] = jnp.where(lo, vsw, zero)
    vm_ref[3, :, :LANES] = jnp.where(lo, zero, v)
    for idx in range(4):
        vm_ref[idx, :, LANES:] = jnp.ones((SEQ, LANES), BF16)
    lo_q = lax.broadcasted_iota(jnp.int32, (1, LANES), 1) < HEAD_DIM

    nh = 2 * (C_WIDTH // LANES)

    def blk(i, carry):
        rows = pl.ds(pl.multiple_of(i * GQ_TQ, GQ_TQ), GQ_TQ)

        def scores(h):
            j, half = divmod(h, 2)
            qj = q_ref[rows, j * LANES:(j + 1) * LANES]
            sel = lo_q if half == 0 else jnp.logical_not(lo_q)
            qm = jnp.where(sel, qj, jnp.zeros_like(qj))
            kk = k_ref[...] if h // 3 == half else ksw_ref[...]
            return lax.dot_general(qm, kk, (((1,), (1,)), ((), ())), preferred_element_type=F32)

        def softmax(s):
            return jnp.exp2(s - jnp.max(s, axis=-1, keepdims=True)).astype(BF16)

        def values(h, pb):
            d = jnp.dot(pb, vm_ref[2 * (h // 3) + h % 2], preferred_element_type=F32)
            return d[:, :LANES] * (1.0 / d[:, LANES:])

        pending = {0: scores(0), 1: scores(1)}
        outs = []
        for h in range(nh):
            pb = softmax(pending.pop(h))
            if h + 2 < nh:
                pending[h + 2] = scores(h + 2)
            outs.append(values(h, pb))
        for j in range(nh // 2):
            o_ref[rows, j * LANES:(j + 1) * LANES] = (outs[2 * j] + outs[2 * j + 1]).astype(BF16)
        return carry

    lax.fori_loop(0, SEQ // GQ_TQ, blk, 0)


def _gqa(qc, kc, vc, batch):
    spec = lambda width: pl.BlockSpec((None, SEQ, width), lambda b: (b, 0, 0))
    out = pl.pallas_call(
        _gqa_kernel, grid=(batch,),
        in_specs=[spec(C_WIDTH), spec(C_KV_WIDTH), spec(C_KV_WIDTH)],
        out_specs=spec(C_WIDTH),
        out_shape=jax.ShapeDtypeStruct((batch, SEQ, C_WIDTH), BF16),
        scratch_shapes=[pltpu.VMEM((SEQ, LANES), BF16), pltpu.VMEM((4, SEQ, 2 * LANES), BF16)],
        compiler_params=_params(), name="gqa",
    )(qc.reshape(batch, SEQ, C_WIDTH), kc.reshape(batch, SEQ, C_KV_WIDTH), vc.reshape(batch, SEQ, C_KV_WIDTH))
    return out.reshape(batch * SEQ, C_WIDTH)


def _filter_kernel(z_ref, trow_ref, w1t_ref, b1_ref, f0_ref, w2t_ref, b2_ref, f1_ref, w3t_ref, dec_ref, out_ref):
    L = SEQ
    dot = functools.partial(jnp.dot, precision=HIGHEST, preferred_element_type=F32)
    h = jnp.sin(f0_ref[...] * (dot(w1t_ref[...], z_ref[...]) + b1_ref[...]))
    h = jnp.sin(f1_ref[...] * (dot(w2t_ref[...], h) + b2_ref[...]))
    trow = trow_ref[...]
    col = lax.broadcasted_iota(jnp.int32, (HY_WIDTH, L), 1)
    for order in range(2):
        fwd = slice(2 * order * HY_WIDTH, (2 * order + 1) * HY_WIDTH)
        bwd = slice((2 * order + 1) * HY_WIDTH, (2 * order + 2) * HY_WIDTH)
        hf = dot(w3t_ref[fwd, :], h[:, L:]) * jnp.exp(-trow[:, L:] * dec_ref[fwd, :])
        hb = dot(w3t_ref[bwd, :], h[:, :L]) * jnp.exp(-trow[:, :L] * dec_ref[bwd, :])
        hb = jnp.where(col == 0, 0.0, pltpu.roll(hb, 1, 1))
        norm = jnp.sum(jnp.abs(hf), axis=-1, keepdims=True) + jnp.sum(jnp.abs(hb), axis=-1, keepdims=True)
        rows = slice(order * HY_WIDTH, (order + 1) * HY_WIDTH)
        out_ref[rows, :L] = hb / norm
        out_ref[rows, L:] = hf / norm


def _filters(zt, trow, w1, b1, freq, w2, b2, w3, decay):
    colv = lambda a: a.reshape(-1, 1).astype(F32)
    w1t = jnp.zeros((HY_HIDDEN, LANES), F32).at[:, :HY_EMB].set(w1.T)
    args = (zt, trow, w1t, colv(b1), colv(freq[0]), w2.T, colv(b2), colv(freq[1]), w3.T, colv(decay))
    return pl.pallas_call(
        _filter_kernel, grid=(1,),
        in_specs=[_resident(a.shape) for a in args],
        out_specs=pl.BlockSpec((2 * HY_WIDTH, 2 * SEQ), lambda _: (0, 0)),
        out_shape=jax.ShapeDtypeStruct((2 * HY_WIDTH, 2 * SEQ), F32),
        compiler_params=_params(), name="filters",
    )(*args)


HY_TB = 256
HY_NB = SEQ // HY_TB


HY_CH = 8


def _hyena_kernel(cw_ref, cb_ref, d_ref, pv_ref, px1_ref, px2_ref, g0_ref, g1_ref, o_ref, gsh_ref, ust_ref, acc_ref):
    c0 = pl.program_id(0) * HY_CH
    nb = pv_ref.shape[1]
    col = lax.broadcasted_iota(jnp.int32, (nb, SEQ), 1)
    chans = range(HY_CH)

    def build_strip(cc, order):
        g = jnp.broadcast_to((g0_ref, g1_ref)[order][cc:cc + 1, :], (16, 2 * SEQ))
        base = pltpu.bitcast(pltpu.roll(g, 0, 1, stride=1, stride_axis=0).astype(BF16), jnp.uint32)
        for a in range(LANES // 16):
            rows = base if a == 0 else pltpu.roll(base, 16 * a, 1)
            gsh_ref[2 * cc + order, 16 * a:16 * (a + 1), :] = pltpu.bitcast(rows, BF16)

    def dwconv(p_ref, cc, ch):
        p = p_ref[cc].astype(F32)
        prev = jnp.where(col == 0, 0.0, pltpu.roll(p, 1, 1))
        nxt = jnp.where(col == SEQ - 1, 0.0, pltpu.roll(p, SEQ - 1, 1))
        return prev * cw_ref[0, ch] + p * cw_ref[1, ch] + nxt * cw_ref[2, ch] + cb_ref[ch]

    def longconv(cc, order, u):
        gsh = gsh_ref.at[2 * cc + order]
        ust, acc = ust_ref.at[cc], acc_ref.at[cc]
        ub = u.astype(BF16)
        for tb in range(HY_NB):
            ust[tb * nb:(tb + 1) * nb, :] = ub[:, tb * HY_TB:(tb + 1) * HY_TB]
        acc[...] = jnp.zeros(acc.shape, F32)
        for dl in range(-(HY_NB - 1), HY_NB):
            x0 = HY_TB * (dl + HY_NB)
            rhs = jnp.concatenate([gsh[:, x0:x0 + HY_TB], gsh[:, x0 - LANES:x0 - LANES + HY_TB]], axis=0)
            b0, b1 = max(0, -dl), min(HY_NB, HY_NB - dl)
            acc[(b0 + dl) * nb:(b1 + dl) * nb, :] += jnp.dot(ust[b0 * nb:b1 * nb, :], rhs, preferred_element_type=F32)
        y = jnp.concatenate([acc[tb * nb:(tb + 1) * nb, :] for tb in range(HY_NB)], axis=1)
        return y + d_ref[order, c0 + cc] * u

    y1, y2 = [], []
    for cc in chans:
        build_strip(cc, 0)
        y1.append(longconv(cc, 0, dwconv(pv_ref, cc, c0 + cc)))
    for cc in chans:
        build_strip(cc, 1)
        y2.append(longconv(cc, 1, dwconv(px1_ref, cc, HY_WIDTH + c0 + cc) * y1[cc]))
    for cc in chans:
        o_ref[cc] = (dwconv(px2_ref, cc, 2 * HY_WIDTH + c0 + cc) * y2[cc]).astype(BF16)


def _hyena(pt, gt, conv_w, conv_b, dbias):
    nb = pt.shape[1]
    smem = pl.BlockSpec(memory_space=pltpu.SMEM)
    per_group = HY_WIDTH // HY_CH
    chan = lambda group: pl.BlockSpec((HY_CH, nb, SEQ), lambda c: (c + group * per_group, 0, 0))
    filt = lambda group: pl.BlockSpec((HY_CH, 2 * SEQ), lambda c: (c + group * per_group, 0))
    return pl.pallas_call(
        _hyena_kernel, grid=(per_group,),
        in_specs=[smem, smem, smem, chan(0), chan(1), chan(2), filt(0), filt(1)],
        out_specs=pl.BlockSpec((HY_CH, nb, SEQ), lambda c: (c, 0, 0)),
        out_shape=jax.ShapeDtypeStruct((HY_WIDTH, nb, SEQ), BF16),
        scratch_shapes=[pltpu.VMEM((2 * HY_CH, LANES, 2 * SEQ), BF16), pltpu.VMEM((HY_CH, HY_NB * nb, HY_TB), BF16),
                        pltpu.VMEM((HY_CH, HY_NB * nb, HY_TB), F32)],
        compiler_params=_params(), name="hyena",
    )(conv_w, conv_b, dbias, pt, pt, pt, gt, gt)


OP_TM = 2048
OP_RB = 512


def _outproj_kernel(oa_ref, hy_ref, oc_ref, w_ref, x_ref, g_ref, out_ref):
    def sub(i, carry):
        blocks = [pl.ds(pl.multiple_of((2 * i + u) * OP_RB, OP_RB), OP_RB) for u in range(2)]
        mixes = [jnp.dot(jnp.concatenate([oa_ref[rows, :], hy_ref[rows, :], oc_ref[rows, :]], axis=1), w_ref[...],
                         preferred_element_type=F32) for rows in blocks]
        for rows, mix in zip(blocks, mixes):
            out_ref[rows, :] = x_ref[rows, :] + _rms(mix, g_ref[...])
        return carry

    lax.fori_loop(0, OP_TM // (2 * OP_RB), sub, 0)


def _outproj(oa, hyo, oc, w, x2, g):
    n = x2.shape[0]
    row = lambda width: pl.BlockSpec((OP_TM, width), lambda i: (i, 0))
    return pl.pallas_call(
        _outproj_kernel, grid=(n // OP_TM,),
        in_specs=[row(A_WIDTH), row(HY_WIDTH), row(C_WIDTH),
                  _resident((D_MODEL, D_MODEL)), row(D_MODEL), _resident((1, D_MODEL))],
        out_specs=row(D_MODEL),
        out_shape=jax.ShapeDtypeStruct((n, D_MODEL), F32),
        compiler_params=_params(), name="outproj",
    )(oa, hyo, oc, w, x2, g)


FF_TM = 1024
FF_RB = 256
FF_HALO = 8


def _ffn_kernel(x_ref, xp_ref, xn_ref, gpre_ref, wg_ref, wu_ref, cw_ref, cb_ref, wd_ref, gpost_ref, out_ref, xs_ref):
    i = pl.program_id(0)
    tiles_per_seq = SEQ // FF_TM
    first = (i % tiles_per_seq) == 0
    last = (i % tiles_per_seq) == tiles_per_seq - 1
    xs_ref[:FF_HALO, :] = jnp.where(first, 0.0, xp_ref[...])
    xs_ref[FF_HALO:FF_HALO + FF_TM, :] = x_ref[...]
    xs_ref[FF_HALO + FF_TM:, :] = jnp.where(last, 0.0, xn_ref[...])

    acts, xs = [], []
    for s in range(FF_TM // FF_RB):
        xh = xs_ref[s * FF_RB:s * FF_RB + FF_RB + 2 * FF_HALO, :]
        x = xh[FF_HALO:FF_HALO + FF_RB]
        gate = jnp.dot(_rms(xh, gpre_ref[...]).astype(BF16), wg_ref[...], preferred_element_type=F32)
        up = jnp.dot(_rms(x, gpre_ref[...]).astype(BF16), wu_ref[...], preferred_element_type=F32)
        gc = (gate[FF_HALO - 1:FF_HALO - 1 + FF_RB] * cw_ref[0:1, :]
              + gate[FF_HALO:FF_HALO + FF_RB] * cw_ref[1:2, :]
              + gate[FF_HALO + 1:FF_HALO + 1 + FF_RB] * cw_ref[2:3, :] + cb_ref[...])
        acts.append((jax.nn.gelu(gc, approximate=True) * up).astype(BF16))
        xs.append(x)
    for s, (act, x) in enumerate(zip(acts, xs)):
        f = jnp.dot(act, wd_ref[...], preferred_element_type=F32)
        out_ref[s * FF_RB:(s + 1) * FF_RB, :] = x + _rms(f, gpost_ref[...])


def _ffn(x2, gpre, wg, wu, cw, cb, wd, gpost):
    n = x2.shape[0]
    per = FF_TM // FF_HALO
    last_blk = n // FF_HALO - 1
    return pl.pallas_call(
        _ffn_kernel, grid=(n // FF_TM,),
        in_specs=[pl.BlockSpec((FF_TM, D_MODEL), lambda i: (i, 0)),
                  pl.BlockSpec((FF_HALO, D_MODEL), lambda i: (jnp.maximum(i * per - 1, 0), 0)),
                  pl.BlockSpec((FF_HALO, D_MODEL), lambda i: (jnp.minimum((i + 1) * per, last_blk), 0)),
                  _resident((1, D_MODEL)), _resident((D_MODEL, D_FF)), _resident((D_MODEL, D_FF)),
                  _resident((3, D_FF)), _resident((1, D_FF)), _resident((D_FF, D_MODEL)), _resident((1, D_MODEL))],
        out_specs=pl.BlockSpec((FF_TM, D_MODEL), lambda i: (i, 0)),
        out_shape=jax.ShapeDtypeStruct((n, D_MODEL), F32),
        scratch_shapes=[pltpu.VMEM((FF_TM + 2 * FF_HALO, D_MODEL), F32)],
        compiler_params=_params(), name="ffn",
    )(x2, x2, x2, gpre, wg, wu, cw, cb, wd, gpost)


def kernel(x, g_mix_pre, g_mix_post, g_ffn_pre, g_ffn_post, w_in, w_out, g_q, g_k, hy_conv_w, hy_conv_b, hy_w1, hy_b1, hy_freq, hy_w2, hy_b2, hy_w3, hy_decay, hy_d, ffn_w_gate, ffn_w_up, ffn_conv_w, ffn_conv_b, ffn_w_down):
    batch = x.shape[0]
    n = batch * SEQ
    tables = _rope_tables()
    bd = _head_mean_matrix()
    zt, trow = _hyena_positions()
    rowv = lambda a: a.reshape(1, -1).astype(F32)
    x2 = x.reshape(n, D_MODEL)
    for i in range(DEPTH):
        qa, ka, va, hy, qc, kc, vc = _inproj(
            x2, rowv(g_mix_pre[i]), w_in[i].astype(BF16), tables,
            rowv(jnp.tile(g_q[i], 2)), rowv(jnp.tile(g_k[i], 2)), bd)
        oa = _dilated(qa, ka, va, batch)
        oc = _gqa(qc, kc, vc, batch)
        gt = _filters(zt, trow, hy_w1[i], hy_b1[i], hy_freq[i], hy_w2[i], hy_b2[i], hy_w3[i], hy_decay[i])
        pt = jnp.transpose(hy.reshape(batch, SEQ, 3 * HY_WIDTH), (2, 0, 1))
        ot = _hyena(pt, gt, hy_conv_w[i].astype(F32), hy_conv_b[i].astype(F32), hy_d[i].astype(F32))
        hyo = jnp.transpose(ot, (1, 2, 0)).reshape(n, HY_WIDTH)
        x2 = _outproj(oa, hyo, oc, w_out[i].astype(BF16), x2, rowv(g_mix_post[i]))
        x2 = _ffn(x2, rowv(g_ffn_pre[i]), ffn_w_gate[i].astype(BF16), ffn_w_up[i].astype(BF16),
                  ffn_conv_w[i].astype(F32), rowv(ffn_conv_b[i]), ffn_w_down[i].astype(BF16),
                  rowv(g_ffn_post[i]))
    return x2.reshape(batch, SEQ, D_MODEL)
```

```python
import functools
import math

import numpy as np
import jax
import jax.numpy as jnp
from jax import lax
from jax.experimental import pallas as pl
from jax.experimental.pallas import tpu as pltpu

F32 = jnp.float32
BF16 = jnp.bfloat16

D_MODEL = 1024
SEQ = 2048
DEPTH = 2
HEAD_DIM = 64
A_WIDTH = 384
HY_WIDTH = 256
C_WIDTH = 384
C_KV_WIDTH = 128
PROJ_WIDTH = 2560
GRID_W = 64
ROPE_THETA = 10000.0
HY_BANDS = 16
HY_EMB = 33
HY_HIDDEN = 64
D_FF = 2816
EPS = 1e-6
N_SIDE = 64
QSCALE = HEAD_DIM ** -0.5 * math.log2(math.e)
NEG = -1e30

LANES = 128
VMEM_LIMIT = 56 * 1024 * 1024

HIGHEST = lax.Precision.HIGHEST


def _params(n_grid_dims=1):
    return pltpu.CompilerParams(
        dimension_semantics=("arbitrary",) * n_grid_dims, vmem_limit_bytes=VMEM_LIMIT)


def _resident(shape):
    nd = len(shape)
    return pl.BlockSpec(shape, lambda *_: (0,) * nd, pipeline_mode=pl.Buffered(1))


def _layer_resident(shape, layer):
    nd = len(shape)
    return pl.BlockSpec((None,) + tuple(shape), lambda *_: (layer,) + (0,) * nd, pipeline_mode=pl.Buffered(1))


def _rope_tables():
    def angles(pos, dim):
        freqs = ROPE_THETA ** (-np.arange(0, dim, 2, dtype=np.float64) / dim)
        ang = pos.astype(np.float64)[:, None] * freqs[None, :]
        return np.cos(ang), np.sin(ang)

    pos = np.arange(SEQ)
    c, s = angles(pos, HEAD_DIM)
    cos_a = np.tile(np.concatenate([c, c], -1), (1, 2))
    sin_a = np.tile(np.concatenate([-s, s], -1), (1, 2))
    cr, sr = angles(pos // GRID_W, HEAD_DIM // 2)
    cc, sc = angles(pos % GRID_W, HEAD_DIM // 2)
    cos_c = np.tile(np.concatenate([cr, cr, cc, cc], -1), (1, 2))
    sin_c = np.tile(np.concatenate([-sr, sr, -sc, sc], -1), (1, 2))
    return tuple(jnp.asarray(t, F32) for t in (cos_a, sin_a, cos_c, sin_c))


def _head_mean_matrix():
    m = np.kron(np.eye(LANES // HEAD_DIM), np.full((HEAD_DIM, HEAD_DIM), 1.0 / HEAD_DIM))
    return jnp.asarray(np.concatenate([m, m], 0), BF16)


def _hyena_positions():
    L = SEQ
    t = np.linspace(0.0, 1.0, L)
    bands = np.linspace(1e-4, HY_BANDS - 1, HY_BANDS)
    ang = 2.0 * math.pi * bands[None, :] * np.arange(L)[:, None] / L
    z = np.concatenate([t[:, None], np.cos(ang), -np.sin(ang)], -1)
    zt = np.zeros((LANES, 2 * L))
    zt[:HY_EMB, :L] = z[::-1].T
    zt[:HY_EMB, L:] = z.T
    trow = np.concatenate([t[::-1], t])[None, :]
    return jnp.asarray(zt, F32), jnp.asarray(trow, F32)


def _rope(x, cos, sin_signed, half, lane):
    first = (lane & (2 * half - 1)) < half
    swapped = jnp.where(first, pltpu.roll(x, LANES - half, 1), pltpu.roll(x, half, 1))
    return x * cos + swapped * sin_signed


def _rms(x, gain):
    ms = jnp.mean(x * x, axis=-1, keepdims=True)
    return x * lax.rsqrt(ms + EPS) * gain


IN_TM = 2048
IN_RB = 512
IN_STEPS = IN_TM // IN_RB


def _inproj_kernel(x_ref, g_ref, w_ref, cosa_ref, sina_ref, cosc_ref, sinc_ref, gq_ref, gk_ref, bd_ref,
                   qa_ref, ka_ref, va_ref, hy_ref, qc_ref, kc_ref, vc_ref, h_ref):
    lane = lax.broadcasted_iota(jnp.int32, (IN_RB, LANES), 1)

    def normed(i):
        rows = pl.ds(pl.multiple_of(i * IN_RB, IN_RB), IN_RB)
        return _rms(x_ref[rows, :], g_ref[...]).astype(BF16)

    h_ref[0] = normed(0)

    def head_norm(blocks, gains):
        hi_lo = jnp.concatenate([jnp.concatenate(_split_hi_lo(v * v), axis=1) for v in blocks], axis=0)
        ms = jnp.dot(hi_lo, bd_ref[...], preferred_element_type=F32)
        return [v * lax.rsqrt(ms[i * IN_RB:(i + 1) * IN_RB] + EPS) * gain
                for i, (v, gain) in enumerate(zip(blocks, gains))]

    def sub(i, carry):
        rows = pl.ds(pl.multiple_of(i * IN_RB, IN_RB), IN_RB)
        slot = i % 2
        h = h_ref[slot]
        h_ref[1 - slot] = normed(jnp.minimum(i + 1, IN_STEPS - 1))
        cosa, sina = cosa_ref[rows, :], sina_ref[rows, :]
        cosc, sinc = cosc_ref[rows, :], sinc_ref[rows, :]
        p = jnp.dot(h, w_ref[...], preferred_element_type=F32)
        block = lambda gb: p[:, gb * LANES:(gb + 1) * LANES]
        normed_qk = head_norm([block(gb) for gb in range(15, 19)], [gq_ref[...]] * 3 + [gk_ref[...]])
        for gb in range(PROJ_WIDTH // LANES):
            blk = block(gb)
            if gb < 3:
                cols = slice(gb * LANES, (gb + 1) * LANES)
                qa_ref[rows, cols] = (_rope(blk, cosa, sina, 32, lane) * QSCALE).astype(BF16)
            elif gb < 6:
                cols = slice((gb - 3) * LANES, (gb - 2) * LANES)
                ka_ref[rows, cols] = _rope(blk, cosa, sina, 32, lane).astype(BF16)
            elif gb < 9:
                cols = slice((gb - 6) * LANES, (gb - 5) * LANES)
                va_ref[rows, cols] = blk.astype(BF16)
            elif gb < 15:
                cols = slice((gb - 9) * LANES, (gb - 8) * LANES)
                hy_ref[rows, cols] = blk.astype(BF16)
            elif gb < 18:
                cols = slice((gb - 15) * LANES, (gb - 14) * LANES)
                q = _rope(normed_qk[gb - 15], cosc, sinc, 16, lane)
                qc_ref[rows, cols] = (q * QSCALE).astype(BF16)
            elif gb == 18:
                kc_ref[rows, :] = _rope(normed_qk[3], cosc, sinc, 16, lane).astype(BF16)
            else:
                vc_ref[rows, :] = blk.astype(BF16)
        return carry

    lax.fori_loop(0, IN_STEPS, sub, 0)


def _inproj(x2, g, w, layer, tables, gq, gk, bd):
    n = x2.shape[0]
    tiles_per_seq = SEQ // IN_TM
    row = lambda width: pl.BlockSpec((IN_TM, width), lambda i: (i, 0))
    tab = pl.BlockSpec((IN_TM, LANES), lambda i: (i % tiles_per_seq, 0))
    outs = [A_WIDTH, A_WIDTH, A_WIDTH, 3 * HY_WIDTH, C_WIDTH, C_KV_WIDTH, C_KV_WIDTH]
    return pl.pallas_call(
        _inproj_kernel,
        grid=(n // IN_TM,),
        in_specs=[row(D_MODEL), _resident((1, D_MODEL)), _layer_resident((D_MODEL, PROJ_WIDTH), layer),
                  tab, tab, tab, tab, _resident((1, LANES)), _resident((1, LANES)), _resident((2 * LANES, LANES))],
        out_specs=[row(wd) for wd in outs],
        out_shape=[jax.ShapeDtypeStruct((n, wd), BF16) for wd in outs],
        scratch_shapes=[pltpu.VMEM((2, IN_RB, D_MODEL), BF16)],
        compiler_params=_params(),
        name="inproj",
    )(x2, g, w, *tables, gq, gk, bd)


DL_BQ = 128
DL_BK = 256
DL_PER_STEP = 2
DL_BLK = 256
DL_NBLK = SEQ // DL_BLK
DL_NPROB = SEQ // DL_BQ


DL_R = 16
DL_SEG = SEQ // DL_R


def _stat_lane(h):
    return HEAD_DIM * (h % 2) + h // 2


def _dilated_constants():
    per = DL_BLK // DL_R
    p = np.zeros((DL_BLK, DL_BLK))
    for i in range(per):
        for r in range(DL_R):
            p[r * per + i, DL_R * i + r] = 1.0
    e = np.zeros((LANES, A_WIDTH))
    for h in range(A_WIDTH // HEAD_DIM):
        e[_stat_lane(h), h * HEAD_DIM:(h + 1) * HEAD_DIM] = 1.0
    return jnp.asarray(p, BF16), jnp.asarray(p.T, BF16), jnp.asarray(np.concatenate([e, e], 0), BF16)


def _band_problems(probs, lane):
    lo = lax.broadcasted_iota(jnp.int32, (1, LANES), 1) < HEAD_DIM
    npair = A_WIDTH // LANES
    scores = []
    for q, k, _, _ in probs:
        nq = q.shape[0]
        ss = []
        for j in range(npair):
            cols = slice(j * LANES, (j + 1) * LANES)
            qj, zero = q[:, cols], jnp.zeros_like(q[:, cols])
            qst = jnp.concatenate([jnp.where(lo, qj, zero), jnp.where(lo, zero, qj)], axis=0)
            s2 = lax.dot_general(qst, k[:, cols], (((1,), (1,)), ((), ())), preferred_element_type=F32)
            ss += [s2[:nq], s2[nq:]]
        scores.append(ss)
    results = []
    for (q, _, v, mask), ss in zip(probs, scores):
        nq, nk = q.shape[0], v.shape[0]
        ones = jnp.ones((nk, LANES), BF16)
        ps = []
        mtile = jnp.zeros((nq, LANES), F32)
        for h, s in enumerate(ss):
            s = s + mask
            m = jnp.max(s, axis=-1, keepdims=True)
            ps.append(jnp.exp2(s - m).astype(BF16))
            mtile = jnp.where(lane == _stat_lane(h), m, mtile)
        outs = []
        ltile = jnp.ones((nq, LANES), F32)
        for j in range(npair):
            vext = jnp.concatenate([v[:, j * LANES:(j + 1) * LANES], ones], axis=1)
            d = jnp.dot(jnp.concatenate(ps[2 * j:2 * j + 2], axis=0), vext, preferred_element_type=F32)
            num0, den0, num1, den1 = d[:nq, :LANES], d[:nq, LANES:], d[nq:, :LANES], d[nq:, LANES:]
            outs.append(jnp.where(lo, num0 * (1.0 / den0), num1 * (1.0 / den1)))
            ltile = jnp.where(lane == j, den0, jnp.where(lane == HEAD_DIM + j, den1, ltile))
        results.append((jnp.concatenate(outs, axis=1), mtile + jnp.log2(ltile)))
    return results


def _split_hi_lo(x):
    hi = x.astype(BF16)
    return hi, (x - hi.astype(F32)).astype(BF16)


def _dilated_kernel(q_ref, k_ref, v_ref, pf_ref, pb_ref, ex_ref, out_ref, xp_ref, ob_ref, l1_ref, st_ref):
    lane = lax.broadcasted_iota(jnp.int32, (DL_BQ, LANES), 1)
    per = DL_BLK // DL_R

    def block_chunks(blk):
        return [pl.ds(pl.multiple_of(r * DL_SEG + blk * per, per), per) for r in range(DL_R)]

    def regroup(i, carry):
        blks = [2 * i, 2 * i + 1]
        ys = []
        for blk in blks:
            rows = pl.ds(pl.multiple_of(blk * DL_BLK, DL_BLK), DL_BLK)
            x = jnp.concatenate([q_ref[rows, :], k_ref[rows, :], v_ref[rows, :]], axis=1)
            ys.append(jnp.dot(pf_ref[...], x, preferred_element_type=F32).astype(BF16))
        for blk, y in zip(blks, ys):
            for r, at in enumerate(block_chunks(blk)):
                for a in range(3):
                    xp_ref[a, at, :] = y[r * per:(r + 1) * per, a * A_WIDTH:(a + 1) * A_WIDTH]
        return carry

    lax.fori_loop(0, DL_NBLK // 2, regroup, 0)

    def run(n_problems, per_step, gather, scatter):
        def step(i, carry):
            ids = [i * per_step + u for u in range(per_step)]
            for p, (o, lse) in zip(ids, _band_problems([gather(p) for p in ids], lane)):
                scatter(p, o.astype(BF16), lse)
            return carry

        lax.fori_loop(0, n_problems // per_step, step, 0)

    def band_mask(diff):
        return jnp.where((diff <= N_SIDE) & (diff >= -N_SIDE), 0.0, NEG)

    def put_regrouped(b, rows, o, lse):
        hi, lo = _split_hi_lo(lse)
        ob_ref[b, rows, :] = o
        st_ref[b - 1, rows, :LANES] = hi
        st_ref[b - 1, rows, LANES:] = lo

    row = lax.broadcasted_iota(jnp.int32, (DL_BQ, DL_BK), 0)
    col = lax.broadcasted_iota(jnp.int32, (DL_BQ, DL_BK), 1)

    def gather1(p):
        t0 = pl.multiple_of(p * DL_BQ, DL_BQ)
        ks = pl.multiple_of(jnp.clip(t0 - N_SIDE, 0, SEQ - DL_BK), N_SIDE)
        keys = pl.ds(ks, DL_BK)
        return q_ref[pl.ds(t0, DL_BQ), :], k_ref[keys, :], v_ref[keys, :], band_mask(col - row + (ks - t0))

    def scatter1(p, o, lse):
        rows = pl.ds(pl.multiple_of(p * DL_BQ, DL_BQ), DL_BQ)
        ob_ref[0, rows, :] = o
        l1_ref[rows, :] = lse

    run(DL_NPROB, DL_PER_STEP, gather1, scatter1)

    n4, qrows, krows = 4, DL_BQ // 4, DL_BK // 4
    slab_of = lambda idx, n: lax.shift_right_logical(idx, n.bit_length() - 1)
    ddiff = 4 * ((col & (krows - 1)) - (row & (qrows - 1))) + (slab_of(col, krows) - slab_of(row, qrows))

    def gather4(p):
        c, b = p // n4, p % n4
        q0 = pl.multiple_of(b * qrows, qrows)
        ks = pl.multiple_of(jnp.clip(q0 - N_SIDE // 4, 0, DL_SEG - krows), N_SIDE // 4)
        slabs = [(c + 4 * s) * DL_SEG for s in range(4)]
        take = lambda a, start, n: jnp.concatenate([xp_ref[a, pl.ds(s0 + start, n), :] for s0 in slabs], axis=0)
        return (take(0, q0, qrows), take(1, ks, krows), take(2, ks, krows), band_mask(ddiff + 4 * (ks - q0)))

    def scatter4(p, o, lse):
        c, b = p // n4, p % n4
        for s in range(4):
            rows = pl.ds(pl.multiple_of((c + 4 * s) * DL_SEG + b * qrows, qrows), qrows)
            put_regrouped(1, rows, o[s * qrows:(s + 1) * qrows], lse[s * qrows:(s + 1) * qrows])

    run(DL_NPROB, DL_PER_STEP, gather4, scatter4)

    mask16 = band_mask(lax.broadcasted_iota(jnp.int32, (DL_BQ, DL_SEG), 1)
                       - lax.broadcasted_iota(jnp.int32, (DL_BQ, DL_SEG), 0))

    def gather16(p):
        rows = pl.ds(pl.multiple_of(p * DL_SEG, DL_SEG), DL_SEG)
        return xp_ref[0, rows, :], xp_ref[1, rows, :], xp_ref[2, rows, :], mask16

    def scatter16(p, o, lse):
        put_regrouped(2, pl.ds(pl.multiple_of(p * DL_SEG, DL_SEG), DL_SEG), o, lse)

    run(DL_R, 2 * DL_PER_STEP, gather16, scatter16)

    def merge(i, carry):
        blks = [2 * i, 2 * i + 1]
        w0 = A_WIDTH + 2 * LANES
        nats = []
        for blk in blks:
            z = jnp.concatenate(
                [jnp.concatenate([ob_ref[1, at, :], st_ref[0, at, :], ob_ref[2, at, :], st_ref[1, at, :]], axis=1)
                 for at in block_chunks(blk)], axis=0)
            nats.append(jnp.dot(pb_ref[...], z, preferred_element_type=F32))
        weights = []
        for blk, nat in zip(blks, nats):
            rows = pl.ds(pl.multiple_of(blk * DL_BLK, DL_BLK), DL_BLK)
            s4 = nat[:, A_WIDTH:A_WIDTH + LANES] + nat[:, A_WIDTH + LANES:w0]
            s16 = nat[:, w0 + A_WIDTH:w0 + A_WIDTH + LANES] + nat[:, w0 + A_WIDTH + LANES:]
            s1 = l1_ref[rows, :]
            mx = jnp.maximum(jnp.maximum(s1, s4), s16)
            e1, e4, e16 = (jnp.exp2(s - mx) for s in (s1, s4, s16))
            inv = 1.0 / (e1 + e4 + e16)
            weights.append((e1 * inv, e4 * inv))
        spread = lambda w: jnp.dot(jnp.concatenate(_split_hi_lo(w), axis=1), ex_ref[...], preferred_element_type=F32)
        spreads = [(spread(a), spread(b)) for a, b in weights]
        for blk, nat, (w1, w4) in zip(blks, nats, spreads):
            rows = pl.ds(pl.multiple_of(blk * DL_BLK, DL_BLK), DL_BLK)
            o1, o4, o16 = ob_ref[0, rows, :].astype(F32), nat[:, :A_WIDTH], nat[:, w0:w0 + A_WIDTH]
            out_ref[rows, :] = (w1 * o1 + w4 * o4 + (1.0 - w1 - w4) * o16).astype(BF16)
        return carry

    lax.fori_loop(0, DL_NBLK // 2, merge, 0)


def _dilated(qa, ka, va, batch):
    seq = pl.BlockSpec((None, SEQ, A_WIDTH), lambda b: (b, 0, 0))
    pf, pb, ex = _dilated_constants()
    shaped = lambda a: a.reshape(batch, SEQ, A_WIDTH)
    out = pl.pallas_call(
        _dilated_kernel, grid=(batch,),
        in_specs=[seq, seq, seq, _resident(pf.shape), _resident(pb.shape), _resident(ex.shape)],
        out_specs=seq,
        out_shape=jax.ShapeDtypeStruct((batch, SEQ, A_WIDTH), BF16),
        scratch_shapes=[pltpu.VMEM((3, SEQ, A_WIDTH), BF16), pltpu.VMEM((3, SEQ, A_WIDTH), BF16),
                        pltpu.VMEM((SEQ, LANES), F32), pltpu.VMEM((2, SEQ, 2 * LANES), BF16)],
        compiler_params=_params(), name="dilated",
    )(shaped(qa), shaped(ka), shaped(va), pf, pb, ex)
    return out.reshape(batch * SEQ, A_WIDTH)


GQ_TQ = 1024


def _gqa_kernel(q_ref, k_ref, v_ref, o_ref, ksw_ref, vm_ref):
    lane = lax.broadcasted_iota(jnp.int32, (SEQ, LANES), 1)
    lo = lane < HEAD_DIM
    k = k_ref[...]
    v = v_ref[...]
    ksw_ref[...] = pltpu.roll(k.astype(F32), HEAD_DIM, 1).astype(BF16)
    vsw = pltpu.roll(v.astype(F32), HEAD_DIM, 1).astype(BF16)
    zero = jnp.zeros_like(v)
    vm_ref[0, :, :LANES] = jnp.where(lo, v, zero)
    vm_ref[1, :, :LANES] = jnp.where(lo, zero, vsw)
    vm_ref[2, :, :LANES] = jnp.where(lo, vsw, zero)
    vm_ref[3, :, :LANES] = jnp.where(lo, zero, v)
    for idx in range(4):
        vm_ref[idx, :, LANES:] = jnp.ones((SEQ, LANES), BF16)
    lo_q = lax.broadcasted_iota(jnp.int32, (1, LANES), 1) < HEAD_DIM

    nh = 2 * (C_WIDTH // LANES)

    def blk(i, carry):
        rows = pl.ds(pl.multiple_of(i * GQ_TQ, GQ_TQ), GQ_TQ)

        def scores(h):
            j, half = divmod(h, 2)
            qj = q_ref[rows, j * LANES:(j + 1) * LANES]
            sel = lo_q if half == 0 else jnp.logical_not(lo_q)
            qm = jnp.where(sel, qj, jnp.zeros_like(qj))
            kk = k_ref[...] if h // 3 == half else ksw_ref[...]
            return lax.dot_general(qm, kk, (((1,), (1,)), ((), ())), preferred_element_type=F32)

        def softmax(s):
            return jnp.exp2(s - jnp.max(s, axis=-1, keepdims=True)).astype(BF16)

        def values(h, pb):
            d = jnp.dot(pb, vm_ref[2 * (h // 3) + h % 2], preferred_element_type=F32)
            return d[:, :LANES] * (1.0 / d[:, LANES:])

        pending = {0: scores(0), 1: scores(1)}
        outs = []
        for h in range(nh):
            pb = softmax(pending.pop(h))
            if h + 2 < nh:
                pending[h + 2] = scores(h + 2)
            outs.append(values(h, pb))
        for j in range(nh // 2):
            o_ref[rows, j * LANES:(j + 1) * LANES] = (outs[2 * j] + outs[2 * j + 1]).astype(BF16)
        return carry

    lax.fori_loop(0, SEQ // GQ_TQ, blk, 0)


def _gqa(qc, kc, vc, batch):
    spec = lambda width: pl.BlockSpec((None, SEQ, width), lambda b: (b, 0, 0))
    out = pl.pallas_call(
        _gqa_kernel, grid=(batch,),
        in_specs=[spec(C_WIDTH), spec(C_KV_WIDTH), spec(C_KV_WIDTH)],
        out_specs=spec(C_WIDTH),
        out_shape=jax.ShapeDtypeStruct((batch, SEQ, C_WIDTH), BF16),
        scratch_shapes=[pltpu.VMEM((SEQ, LANES), BF16), pltpu.VMEM((4, SEQ, 2 * LANES), BF16)],
        compiler_params=_params(), name="gqa",
    )(qc.reshape(batch, SEQ, C_WIDTH), kc.reshape(batch, SEQ, C_KV_WIDTH), vc.reshape(batch, SEQ, C_KV_WIDTH))
    return out.reshape(batch * SEQ, C_WIDTH)


def _filter_kernel(z_ref, trow_ref, w1t_ref, b1_ref, f0_ref, w2t_ref, b2_ref, f1_ref, w3t_ref, dec_ref, out_ref):
    L = SEQ
    dot = functools.partial(jnp.dot, precision=HIGHEST, preferred_element_type=F32)
    h = jnp.sin(f0_ref[...] * (dot(w1t_ref[...], z_ref[...]) + b1_ref[...]))
    h = jnp.sin(f1_ref[...] * (dot(w2t_ref[...], h) + b2_ref[...]))
    trow = trow_ref[...]
    col = lax.broadcasted_iota(jnp.int32, (HY_WIDTH, L), 1)
    for order in range(2):
        fwd = slice(2 * order * HY_WIDTH, (2 * order + 1) * HY_WIDTH)
        bwd = slice((2 * order + 1) * HY_WIDTH, (2 * order + 2) * HY_WIDTH)
        hf = dot(w3t_ref[fwd, :], h[:, L:]) * jnp.exp(-trow[:, L:] * dec_ref[fwd, :])
        hb = dot(w3t_ref[bwd, :], h[:, :L]) * jnp.exp(-trow[:, :L] * dec_ref[bwd, :])
        hb = jnp.where(col == 0, 0.0, pltpu.roll(hb, 1, 1))
        norm = jnp.sum(jnp.abs(hf), axis=-1, keepdims=True) + jnp.sum(jnp.abs(hb), axis=-1, keepdims=True)
        rows = slice(order * HY_WIDTH, (order + 1) * HY_WIDTH)
        out_ref[rows, :L] = hb / norm
        out_ref[rows, L:] = hf / norm


def _filters(zt, trow, w1, b1, freq, w2, b2, w3, decay):
    colv = lambda a: a.reshape(-1, 1).astype(F32)
    w1t = jnp.zeros((HY_HIDDEN, LANES), F32).at[:, :HY_EMB].set(w1.T)
    args = (zt, trow, w1t, colv(b1), colv(freq[0]), w2.T, colv(b2), colv(freq[1]), w3.T, colv(decay))
    return pl.pallas_call(
        _filter_kernel, grid=(1,),
        in_specs=[_resident(a.shape) for a in args],
        out_specs=pl.BlockSpec((2 * HY_WIDTH, 2 * SEQ), lambda _: (0, 0)),
        out_shape=jax.ShapeDtypeStruct((2 * HY_WIDTH, 2 * SEQ), F32),
        compiler_params=_params(), name="filters",
    )(*args)


HY_TB = 256
HY_NB = SEQ // HY_TB


HY_CH = 8


def _hyena_kernel(cw_ref, cb_ref, d_ref, pv_ref, px1_ref, px2_ref, g0_ref, g1_ref, o_ref, gsh_ref, ust_ref, acc_ref):
    c0 = pl.program_id(0) * HY_CH
    nb = pv_ref.shape[1]
    col = lax.broadcasted_iota(jnp.int32, (nb, SEQ), 1)
    chans = range(HY_CH)

    def build_strip(cc, order):
        g = jnp.broadcast_to((g0_ref, g1_ref)[order][cc:cc + 1, :], (16, 2 * SEQ))
        base = pltpu.bitcast(pltpu.roll(g, 0, 1, stride=1, stride_axis=0).astype(BF16), jnp.uint32)
        for a in range(LANES // 16):
            rows = base if a == 0 else pltpu.roll(base, 16 * a, 1)
            gsh_ref[2 * cc + order, 16 * a:16 * (a + 1), :] = pltpu.bitcast(rows, BF16)

    def dwconv(p_ref, cc, ch):
        p = p_ref[cc].astype(F32)
        prev = jnp.where(col == 0, 0.0, pltpu.roll(p, 1, 1))
        nxt = jnp.where(col == SEQ - 1, 0.0, pltpu.roll(p, SEQ - 1, 1))
        return prev * cw_ref[0, ch] + p * cw_ref[1, ch] + nxt * cw_ref[2, ch] + cb_ref[ch]

    def longconv(cc, order, u):
        gsh = gsh_ref.at[2 * cc + order]
        ust, acc = ust_ref.at[cc], acc_ref.at[cc]
        ub = u.astype(BF16)
        for tb in range(HY_NB):
            ust[tb * nb:(tb + 1) * nb, :] = ub[:, tb * HY_TB:(tb + 1) * HY_TB]
        acc[...] = jnp.zeros(acc.shape, F32)
        for dl in range(-(HY_NB - 1), HY_NB):
            x0 = HY_TB * (dl + HY_NB)
            rhs = jnp.concatenate([gsh[:, x0:x0 + HY_TB], gsh[:, x0 - LANES:x0 - LANES + HY_TB]], axis=0)
            b0, b1 = max(0, -dl), min(HY_NB, HY_NB - dl)
            acc[(b0 + dl) * nb:(b1 + dl) * nb, :] += jnp.dot(ust[b0 * nb:b1 * nb, :], rhs, preferred_element_type=F32)
        y = jnp.concatenate([acc[tb * nb:(tb + 1) * nb, :] for tb in range(HY_NB)], axis=1)
        return y + d_ref[order, c0 + cc] * u

    y1, y2 = [], []
    for cc in chans:
        build_strip(cc, 0)
        y1.append(longconv(cc, 0, dwconv(pv_ref, cc, c0 + cc)))
    for cc in chans:
        build_strip(cc, 1)
        y2.append(longconv(cc, 1, dwconv(px1_ref, cc, HY_WIDTH + c0 + cc) * y1[cc]))
    for cc in chans:
        o_ref[cc] = (dwconv(px2_ref, cc, 2 * HY_WIDTH + c0 + cc) * y2[cc]).astype(BF16)


def _hyena(pt, gt, conv_w, conv_b, dbias):
    nb = pt.shape[1]
    smem = pl.BlockSpec(memory_space=pltpu.SMEM)
    per_group = HY_WIDTH // HY_CH
    chan = lambda group: pl.BlockSpec((HY_CH, nb, SEQ), lambda c: (c + group * per_group, 0, 0))
    filt = lambda group: pl.BlockSpec((HY_CH, 2 * SEQ), lambda c: (c + group * per_group, 0))
    return pl.pallas_call(
        _hyena_kernel, grid=(per_group,),
        in_specs=[smem, smem, smem, chan(0), chan(1), chan(2), filt(0), filt(1)],
        out_specs=pl.BlockSpec((HY_CH, nb, SEQ), lambda c: (c, 0, 0)),
        out_shape=jax.ShapeDtypeStruct((HY_WIDTH, nb, SEQ), BF16),
        scratch_shapes=[pltpu.VMEM((2 * HY_CH, LANES, 2 * SEQ), BF16), pltpu.VMEM((HY_CH, HY_NB * nb, HY_TB), BF16),
                        pltpu.VMEM((HY_CH, HY_NB * nb, HY_TB), F32)],
        compiler_params=_params(), name="hyena",
    )(conv_w, conv_b, dbias, pt, pt, pt, gt, gt)


OP_TM = 2048
OP_RB = 512


def _outproj_kernel(oa_ref, hy_ref, oc_ref, w_ref, x_ref, g_ref, out_ref):
    def sub(i, carry):
        blocks = [pl.ds(pl.multiple_of((2 * i + u) * OP_RB, OP_RB), OP_RB) for u in range(2)]
        mixes = [jnp.dot(jnp.concatenate([oa_ref[rows, :], hy_ref[rows, :], oc_ref[rows, :]], axis=1), w_ref[...],
                         preferred_element_type=F32) for rows in blocks]
        for rows, mix in zip(blocks, mixes):
            out_ref[rows, :] = x_ref[rows, :] + _rms(mix, g_ref[...])
        return carry

    lax.fori_loop(0, OP_TM // (2 * OP_RB), sub, 0)


def _outproj(oa, hyo, oc, w, layer, x2, g):
    n = x2.shape[0]
    row = lambda width: pl.BlockSpec((OP_TM, width), lambda i: (i, 0))
    return pl.pallas_call(
        _outproj_kernel, grid=(n // OP_TM,),
        in_specs=[row(A_WIDTH), row(HY_WIDTH), row(C_WIDTH),
                  _layer_resident((D_MODEL, D_MODEL), layer), row(D_MODEL), _resident((1, D_MODEL))],
        out_specs=row(D_MODEL),
        out_shape=jax.ShapeDtypeStruct((n, D_MODEL), F32),
        compiler_params=_params(), name="outproj",
    )(oa, hyo, oc, w, x2, g)


FF_TM = 1024
FF_RB = 256
FF_HALO = 8


def _ffn_kernel(x_ref, xp_ref, xn_ref, gpre_ref, wg_ref, wu_ref, cw_ref, cb_ref, wd_ref, gpost_ref, out_ref, xs_ref):
    i = pl.program_id(0)
    tiles_per_seq = SEQ // FF_TM
    first = (i % tiles_per_seq) == 0
    last = (i % tiles_per_seq) == tiles_per_seq - 1
    xs_ref[:FF_HALO, :] = jnp.where(first, 0.0, xp_ref[...])
    xs_ref[FF_HALO:FF_HALO + FF_TM, :] = x_ref[...]
    xs_ref[FF_HALO + FF_TM:, :] = jnp.where(last, 0.0, xn_ref[...])

    acts, xs = [], []
    for s in range(FF_TM // FF_RB):
        xh = xs_ref[s * FF_RB:s * FF_RB + FF_RB + 2 * FF_HALO, :]
        x = xh[FF_HALO:FF_HALO + FF_RB]
        gate = jnp.dot(_rms(xh, gpre_ref[...]).astype(BF16), wg_ref[...], preferred_element_type=F32)
        up = jnp.dot(_rms(x, gpre_ref[...]).astype(BF16), wu_ref[...], preferred_element_type=F32)
        gc = (gate[FF_HALO - 1:FF_HALO - 1 + FF_RB] * cw_ref[0:1, :]
              + gate[FF_HALO:FF_HALO + FF_RB] * cw_ref[1:2, :]
              + gate[FF_HALO + 1:FF_HALO + 1 + FF_RB] * cw_ref[2:3, :] + cb_ref[...])
        acts.append((jax.nn.gelu(gc, approximate=True) * up).astype(BF16))
        xs.append(x)
    for s, (act, x) in enumerate(zip(acts, xs)):
        f = jnp.dot(act, wd_ref[...], preferred_element_type=F32)
        out_ref[s * FF_RB:(s + 1) * FF_RB, :] = x + _rms(f, gpost_ref[...])


def _ffn(x2, gpre, wg, wu, cw, cb, wd, gpost, layer):
    n = x2.shape[0]
    per = FF_TM // FF_HALO
    last_blk = n // FF_HALO - 1
    return pl.pallas_call(
        _ffn_kernel, grid=(n // FF_TM,),
        in_specs=[pl.BlockSpec((FF_TM, D_MODEL), lambda i: (i, 0)),
                  pl.BlockSpec((FF_HALO, D_MODEL), lambda i: (jnp.maximum(i * per - 1, 0), 0)),
                  pl.BlockSpec((FF_HALO, D_MODEL), lambda i: (jnp.minimum((i + 1) * per, last_blk), 0)),
                  _resident((1, D_MODEL)), _layer_resident((D_MODEL, D_FF), layer),
                  _layer_resident((D_MODEL, D_FF), layer), _resident((3, D_FF)), _resident((1, D_FF)),
                  _layer_resident((D_FF, D_MODEL), layer), _resident((1, D_MODEL))],
        out_specs=pl.BlockSpec((FF_TM, D_MODEL), lambda i: (i, 0)),
        out_shape=jax.ShapeDtypeStruct((n, D_MODEL), F32),
        scratch_shapes=[pltpu.VMEM((FF_TM + 2 * FF_HALO, D_MODEL), F32)],
        compiler_params=_params(), name="ffn",
    )(x2, x2, x2, gpre, wg, wu, cw, cb, wd, gpost)


def kernel(x, g_mix_pre, g_mix_post, g_ffn_pre, g_ffn_post, w_in, w_out, g_q, g_k, hy_conv_w, hy_conv_b, hy_w1, hy_b1, hy_freq, hy_w2, hy_b2, hy_w3, hy_decay, hy_d, ffn_w_gate, ffn_w_up, ffn_conv_w, ffn_conv_b, ffn_w_down):
    batch = x.shape[0]
    n = batch * SEQ
    tables = _rope_tables()
    bd = _head_mean_matrix()
    zt, trow = _hyena_positions()
    rowv = lambda a: a.reshape(1, -1).astype(F32)
    x2 = x.reshape(n, D_MODEL)
    w_in, w_out, ffn_w_gate, ffn_w_up, ffn_w_down = (
        w.astype(BF16) for w in (w_in, w_out, ffn_w_gate, ffn_w_up, ffn_w_down))
    for i in range(DEPTH):
        qa, ka, va, hy, qc, kc, vc = _inproj(
            x2, rowv(g_mix_pre[i]), w_in, i, tables,
            rowv(jnp.tile(g_q[i], 2)), rowv(jnp.tile(g_k[i], 2)), bd)
        oa = _dilated(qa, ka, va, batch)
        oc = _gqa(qc, kc, vc, batch)
        gt = _filters(zt, trow, hy_w1[i], hy_b1[i], hy_freq[i], hy_w2[i], hy_b2[i], hy_w3[i], hy_decay[i])
        pt = jnp.transpose(hy.reshape(batch, SEQ, 3 * HY_WIDTH), (2, 0, 1))
        ot = _hyena(pt, gt, hy_conv_w[i].astype(F32), hy_conv_b[i].astype(F32), hy_d[i].astype(F32))
        hyo = jnp.transpose(ot, (1, 2, 0)).reshape(n, HY_WIDTH)
        x2 = _outproj(oa, hyo, oc, w_out, i, x2, rowv(g_mix_post[i]))
        x2 = _ffn(x2, rowv(g_ffn_pre[i]), ffn_w_gate, ffn_w_up, ffn_conv_w[i].astype(F32), rowv(ffn_conv_b[i]),
                  ffn_w_down, rowv(g_ffn_post[i]), i)
    return x2.reshape(batch, SEQ, D_MODEL)
```

```python
import functools
import math

import numpy as np
import jax
import jax.numpy as jnp
from jax import lax
from jax.experimental import pallas as pl
from jax.experimental.pallas import tpu as pltpu

F32 = jnp.float32
BF16 = jnp.bfloat16

D_MODEL = 1024
SEQ = 2048
DEPTH = 2
HEAD_DIM = 64
A_WIDTH = 384
HY_WIDTH = 256
C_WIDTH = 384
C_KV_WIDTH = 128
PROJ_WIDTH = 2560
GRID_W = 64
ROPE_THETA = 10000.0
HY_BANDS = 16
HY_EMB = 33
HY_HIDDEN = 64
D_FF = 2816
EPS = 1e-6
N_SIDE = 64
QSCALE = HEAD_DIM ** -0.5 * math.log2(math.e)
NEG = -1e30

LANES = 128
BF16_ROWS = 16
VMEM_LIMIT = 56 * 1024 * 1024

HIGHEST = lax.Precision.HIGHEST


def _params(n_grid_dims=1):
    return pltpu.CompilerParams(
        dimension_semantics=("arbitrary",) * n_grid_dims, vmem_limit_bytes=VMEM_LIMIT)


def _resident(shape):
    nd = len(shape)
    return pl.BlockSpec(shape, lambda *_: (0,) * nd, pipeline_mode=pl.Buffered(1))


def _layer_resident(shape, layer):
    nd = len(shape)
    return pl.BlockSpec((None,) + tuple(shape), lambda *_: (layer,) + (0,) * nd, pipeline_mode=pl.Buffered(1))


def _rope_tables():
    def angles(pos, dim):
        freqs = ROPE_THETA ** (-np.arange(0, dim, 2, dtype=np.float64) / dim)
        ang = pos.astype(np.float64)[:, None] * freqs[None, :]
        return np.cos(ang), np.sin(ang)

    pos = np.arange(SEQ)
    c, s = angles(pos, HEAD_DIM)
    cos_a = np.tile(np.concatenate([c, c], -1), (1, 2))
    sin_a = np.tile(np.concatenate([-s, s], -1), (1, 2))
    cr, sr = angles(pos // GRID_W, HEAD_DIM // 2)
    cc, sc = angles(pos % GRID_W, HEAD_DIM // 2)
    cos_c = np.tile(np.concatenate([cr, cr, cc, cc], -1), (1, 2))
    sin_c = np.tile(np.concatenate([-sr, sr, -sc, sc], -1), (1, 2))
    return tuple(jnp.asarray(t, F32) for t in (cos_a, sin_a, cos_c, sin_c))


def _head_mean_matrix():
    m = np.kron(np.eye(LANES // HEAD_DIM), np.full((HEAD_DIM, HEAD_DIM), 1.0 / HEAD_DIM))
    return jnp.asarray(np.concatenate([m, m], 0), BF16)


def _hyena_positions():
    L = SEQ
    t = np.linspace(0.0, 1.0, L)
    bands = np.linspace(1e-4, HY_BANDS - 1, HY_BANDS)
    ang = 2.0 * math.pi * bands[None, :] * np.arange(L)[:, None] / L
    z = np.concatenate([t[:, None], np.cos(ang), -np.sin(ang)], -1)
    zt = np.zeros((LANES, 2 * L))
    zt[:HY_EMB, :L] = z[::-1].T
    zt[:HY_EMB, L:] = z.T
    trow = np.concatenate([t[::-1], t])[None, :]
    return jnp.asarray(zt, F32), jnp.asarray(trow, F32)


def _rope(x, cos, sin_signed, half, lane):
    first = (lane & (2 * half - 1)) < half
    swapped = jnp.where(first, pltpu.roll(x, LANES - half, 1), pltpu.roll(x, half, 1))
    return x * cos + swapped * sin_signed


def _rms(x, gain):
    ms = jnp.mean(x * x, axis=-1, keepdims=True)
    return x * lax.rsqrt(ms + EPS) * gain


IN_TM = 2048
IN_RB = 512
IN_STEPS = IN_TM // IN_RB


def _inproj_kernel(x_ref, g_ref, w_ref, cosa_ref, sina_ref, cosc_ref, sinc_ref, gq_ref, gk_ref, bd_ref,
                   qa_ref, ka_ref, va_ref, hy_ref, qc_ref, kc_ref, vc_ref, h_ref):
    lane = lax.broadcasted_iota(jnp.int32, (IN_RB, LANES), 1)

    def normed(i):
        rows = pl.ds(pl.multiple_of(i * IN_RB, IN_RB), IN_RB)
        return _rms(x_ref[rows, :], g_ref[...]).astype(BF16)

    h_ref[0] = normed(0)

    def head_norm(blocks, gains):
        hi_lo = jnp.concatenate([jnp.concatenate(_split_hi_lo(v * v), axis=1) for v in blocks], axis=0)
        ms = jnp.dot(hi_lo, bd_ref[...], preferred_element_type=F32)
        return [v * lax.rsqrt(ms[i * IN_RB:(i + 1) * IN_RB] + EPS) * gain
                for i, (v, gain) in enumerate(zip(blocks, gains))]

    def sub(i, carry):
        rows = pl.ds(pl.multiple_of(i * IN_RB, IN_RB), IN_RB)
        slot = i % 2
        h = h_ref[slot]
        h_ref[1 - slot] = normed(jnp.minimum(i + 1, IN_STEPS - 1))
        cosa, sina = cosa_ref[rows, :], sina_ref[rows, :]
        cosc, sinc = cosc_ref[rows, :], sinc_ref[rows, :]
        half = PROJ_WIDTH // 2
        p1 = jnp.dot(h, w_ref[:, :half], preferred_element_type=F32)
        blk1 = lambda b: p1[:, b * LANES:(b + 1) * LANES]
        normed_qk = head_norm([blk1(b) for b in range(4)], [gq_ref[...]] * 3 + [gk_ref[...]])
        p2 = jnp.dot(h, w_ref[:, half:], preferred_element_type=F32)
        blk2 = lambda b: p2[:, b * LANES:(b + 1) * LANES]
        for b in range(A_WIDTH // LANES):
            cols = slice(b * LANES, (b + 1) * LANES)
            qc_ref[rows, cols] = (_rope(normed_qk[b], cosc, sinc, 16, lane) * QSCALE).astype(BF16)
            qa_ref[rows, cols] = (_rope(blk1(4 + b), cosa, sina, 32, lane) * QSCALE).astype(BF16)
            ka_ref[rows, cols] = _rope(blk1(7 + b), cosa, sina, 32, lane).astype(BF16)
            va_ref[rows, cols] = blk2(b).astype(BF16)
        kc_ref[rows, :] = _rope(normed_qk[3], cosc, sinc, 16, lane).astype(BF16)
        for b in range(3 * HY_WIDTH // LANES):
            hy_ref[rows, b * LANES:(b + 1) * LANES] = blk2(3 + b).astype(BF16)
        vc_ref[rows, :] = blk2(9).astype(BF16)
        return carry

    lax.fori_loop(0, IN_STEPS, sub, 0)


def _inproj_column_order(w):
    a_end, hy_end = 3 * A_WIDTH, 3 * A_WIDTH + 3 * HY_WIDTH
    c_qk_end = hy_end + C_WIDTH + C_KV_WIDTH
    return jnp.concatenate([w[..., hy_end:c_qk_end], w[..., :2 * A_WIDTH], w[..., 2 * A_WIDTH:hy_end],
                            w[..., c_qk_end:]], axis=-1)


def _inproj(x2, g, w, layer, tables, gq, gk, bd):
    n = x2.shape[0]
    tiles_per_seq = SEQ // IN_TM
    row = lambda width: pl.BlockSpec((IN_TM, width), lambda i: (i, 0))
    tab = pl.BlockSpec((IN_TM, LANES), lambda i: (i % tiles_per_seq, 0))
    outs = [A_WIDTH, A_WIDTH, A_WIDTH, 3 * HY_WIDTH, C_WIDTH, C_KV_WIDTH, C_KV_WIDTH]
    return pl.pallas_call(
        _inproj_kernel,
        grid=(n // IN_TM,),
        in_specs=[row(D_MODEL), _resident((1, D_MODEL)), _layer_resident((D_MODEL, PROJ_WIDTH), layer),
                  tab, tab, tab, tab, _resident((1, LANES)), _resident((1, LANES)), _resident((2 * LANES, LANES))],
        out_specs=[row(wd) for wd in outs],
        out_shape=[jax.ShapeDtypeStruct((n, wd), BF16) for wd in outs],
        scratch_shapes=[pltpu.VMEM((2, IN_RB, D_MODEL), BF16)],
        compiler_params=_params(),
        name="inproj",
    )(x2, g, w, *tables, gq, gk, bd)


DL_BQ = 128
DL_BK = 256
DL_PER_STEP = 2
DL_BLK = 256
DL_NBLK = SEQ // DL_BLK
DL_NPROB = SEQ // DL_BQ


DL_R = 16
DL_SEG = SEQ // DL_R


def _stat_lane(h):
    return HEAD_DIM * (h % 2) + h // 2


def _dilated_constants():
    per = DL_BLK // DL_R
    p = np.zeros((DL_BLK, DL_BLK))
    for i in range(per):
        for r in range(DL_R):
            p[r * per + i, DL_R * i + r] = 1.0
    e = np.zeros((LANES, A_WIDTH))
    for h in range(A_WIDTH // HEAD_DIM):
        e[_stat_lane(h), h * HEAD_DIM:(h + 1) * HEAD_DIM] = 1.0
    return jnp.asarray(p, BF16), jnp.asarray(p.T, BF16), jnp.asarray(np.concatenate([e, e], 0), BF16)


def _band_problems(probs, lane):
    lo = lax.broadcasted_iota(jnp.int32, (1, LANES), 1) < HEAD_DIM
    npair = A_WIDTH // LANES
    scores = []
    for q, k, _, _ in probs:
        nq = q.shape[0]
        ss = []
        for j in range(npair):
            cols = slice(j * LANES, (j + 1) * LANES)
            qj, zero = q[:, cols], jnp.zeros_like(q[:, cols])
            qst = jnp.concatenate([jnp.where(lo, qj, zero), jnp.where(lo, zero, qj)], axis=0)
            s2 = lax.dot_general(qst, k[:, cols], (((1,), (1,)), ((), ())), preferred_element_type=F32)
            ss += [s2[:nq], s2[nq:]]
        scores.append(ss)
    results = []
    for (q, _, v, mask), ss in zip(probs, scores):
        nq, nk = q.shape[0], v.shape[0]
        ones = jnp.ones((nk, LANES), BF16)
        ps = []
        mtile = jnp.zeros((nq, LANES), F32)
        for h, s in enumerate(ss):
            s = s + mask
            m = jnp.max(s, axis=-1, keepdims=True)
            ps.append(jnp.exp2(s - m).astype(BF16))
            mtile = jnp.where(lane == _stat_lane(h), m, mtile)
        outs = []
        ltile = jnp.ones((nq, LANES), F32)
        for j in range(npair):
            vext = jnp.concatenate([v[:, j * LANES:(j + 1) * LANES], ones], axis=1)
            d = jnp.dot(jnp.concatenate(ps[2 * j:2 * j + 2], axis=0), vext, preferred_element_type=F32)
            num0, den0, num1, den1 = d[:nq, :LANES], d[:nq, LANES:], d[nq:, :LANES], d[nq:, LANES:]
            outs.append(jnp.where(lo, num0 * (1.0 / den0), num1 * (1.0 / den1)))
            ltile = jnp.where(lane == j, den0, jnp.where(lane == HEAD_DIM + j, den1, ltile))
        results.append((jnp.concatenate(outs, axis=1), mtile + jnp.log2(ltile)))
    return results


def _split_hi_lo(x):
    hi = x.astype(BF16)
    return hi, (x - hi.astype(F32)).astype(BF16)


def _dilated_kernel(q_ref, k_ref, v_ref, pf_ref, pb_ref, ex_ref, out_ref, xp_ref, ob_ref, l1_ref, st_ref):
    lane = lax.broadcasted_iota(jnp.int32, (DL_BQ, LANES), 1)
    per = DL_BLK // DL_R

    def block_chunks(blk):
        return [pl.ds(pl.multiple_of(r * DL_SEG + blk * per, per), per) for r in range(DL_R)]

    def regroup(i, carry):
        blks = [2 * i, 2 * i + 1]
        ys = []
        for blk in blks:
            rows = pl.ds(pl.multiple_of(blk * DL_BLK, DL_BLK), DL_BLK)
            x = jnp.concatenate([q_ref[rows, :], k_ref[rows, :], v_ref[rows, :]], axis=1)
            ys.append(jnp.dot(pf_ref[...], x, preferred_element_type=F32).astype(BF16))
        for blk, y in zip(blks, ys):
            for r, at in enumerate(block_chunks(blk)):
                for a in range(3):
                    xp_ref[a, at, :] = y[r * per:(r + 1) * per, a * A_WIDTH:(a + 1) * A_WIDTH]
        return carry

    lax.fori_loop(0, DL_NBLK // 2, regroup, 0)

    def run(n_problems, per_step, gather, scatter):
        def step(i, carry):
            ids = [i * per_step + u for u in range(per_step)]
            for p, (o, lse) in zip(ids, _band_problems([gather(p) for p in ids], lane)):
                scatter(p, o.astype(BF16), lse)
            return carry

        lax.fori_loop(0, n_problems // per_step, step, 0)

    def band_mask(diff):
        return jnp.where((diff <= N_SIDE) & (diff >= -N_SIDE), 0.0, NEG)

    def put_regrouped(b, rows, o, lse):
        hi, lo = _split_hi_lo(lse)
        ob_ref[b, rows, :] = o
        st_ref[b - 1, rows, :LANES] = hi
        st_ref[b - 1, rows, LANES:] = lo

    row = lax.broadcasted_iota(jnp.int32, (DL_BQ, DL_BK), 0)
    col = lax.broadcasted_iota(jnp.int32, (DL_BQ, DL_BK), 1)

    def gather1(p):
        t0 = pl.multiple_of(p * DL_BQ, DL_BQ)
        ks = pl.multiple_of(jnp.clip(t0 - N_SIDE, 0, SEQ - DL_BK), N_SIDE)
        keys = pl.ds(ks, DL_BK)
        return q_ref[pl.ds(t0, DL_BQ), :], k_ref[keys, :], v_ref[keys, :], band_mask(col - row + (ks - t0))

    def scatter1(p, o, lse):
        rows = pl.ds(pl.multiple_of(p * DL_BQ, DL_BQ), DL_BQ)
        ob_ref[0, rows, :] = o
        l1_ref[rows, :] = lse

    run(DL_NPROB, DL_PER_STEP, gather1, scatter1)

    n4, qrows, krows = 4, DL_BQ // 4, DL_BK // 4
    slab_of = lambda idx, n: lax.shift_right_logical(idx, n.bit_length() - 1)
    ddiff = 4 * ((col & (krows - 1)) - (row & (qrows - 1))) + (slab_of(col, krows) - slab_of(row, qrows))

    def gather4(p):
        c, b = p // n4, p % n4
        q0 = pl.multiple_of(b * qrows, qrows)
        ks = pl.multiple_of(jnp.clip(q0 - N_SIDE // 4, 0, DL_SEG - krows), N_SIDE // 4)
        slabs = [(c + 4 * s) * DL_SEG for s in range(4)]
        take = lambda a, start, n: jnp.concatenate([xp_ref[a, pl.ds(s0 + start, n), :] for s0 in slabs], axis=0)
        return (take(0, q0, qrows), take(1, ks, krows), take(2, ks, krows), band_mask(ddiff + 4 * (ks - q0)))

    def scatter4(p, o, lse):
        c, b = p // n4, p % n4
        for s in range(4):
            rows = pl.ds(pl.multiple_of((c + 4 * s) * DL_SEG + b * qrows, qrows), qrows)
            put_regrouped(1, rows, o[s * qrows:(s + 1) * qrows], lse[s * qrows:(s + 1) * qrows])

    run(DL_NPROB, DL_PER_STEP, gather4, scatter4)

    mask16 = band_mask(lax.broadcasted_iota(jnp.int32, (DL_BQ, DL_SEG), 1)
                       - lax.broadcasted_iota(jnp.int32, (DL_BQ, DL_SEG), 0))

    def gather16(p):
        rows = pl.ds(pl.multiple_of(p * DL_SEG, DL_SEG), DL_SEG)
        return xp_ref[0, rows, :], xp_ref[1, rows, :], xp_ref[2, rows, :], mask16

    def scatter16(p, o, lse):
        put_regrouped(2, pl.ds(pl.multiple_of(p * DL_SEG, DL_SEG), DL_SEG), o, lse)

    run(DL_R, 2 * DL_PER_STEP, gather16, scatter16)

    def merge(i, carry):
        blks = [2 * i, 2 * i + 1]
        w0 = A_WIDTH + 2 * LANES
        nats = []
        for blk in blks:
            z = jnp.concatenate(
                [jnp.concatenate([ob_ref[1, at, :], st_ref[0, at, :], ob_ref[2, at, :], st_ref[1, at, :]], axis=1)
                 for at in block_chunks(blk)], axis=0)
            nats.append(jnp.dot(pb_ref[...], z, preferred_element_type=F32))
        weights = []
        for blk, nat in zip(blks, nats):
            rows = pl.ds(pl.multiple_of(blk * DL_BLK, DL_BLK), DL_BLK)
            s4 = nat[:, A_WIDTH:A_WIDTH + LANES] + nat[:, A_WIDTH + LANES:w0]
            s16 = nat[:, w0 + A_WIDTH:w0 + A_WIDTH + LANES] + nat[:, w0 + A_WIDTH + LANES:]
            s1 = l1_ref[rows, :]
            mx = jnp.maximum(jnp.maximum(s1, s4), s16)
            e1, e4, e16 = (jnp.exp2(s - mx) for s in (s1, s4, s16))
            inv = 1.0 / (e1 + e4 + e16)
            weights.append((e1 * inv, e4 * inv))
        spread = lambda w: jnp.dot(jnp.concatenate(_split_hi_lo(w), axis=1), ex_ref[...], preferred_element_type=F32)
        spreads = [(spread(a), spread(b)) for a, b in weights]
        for blk, nat, (w1, w4) in zip(blks, nats, spreads):
            rows = pl.ds(pl.multiple_of(blk * DL_BLK, DL_BLK), DL_BLK)
            o1, o4, o16 = ob_ref[0, rows, :].astype(F32), nat[:, :A_WIDTH], nat[:, w0:w0 + A_WIDTH]
            out_ref[rows, :] = (w1 * o1 + w4 * o4 + (1.0 - w1 - w4) * o16).astype(BF16)
        return carry

    lax.fori_loop(0, DL_NBLK // 2, merge, 0)


def _dilated(qa, ka, va, batch):
    seq = pl.BlockSpec((None, SEQ, A_WIDTH), lambda b: (b, 0, 0))
    pf, pb, ex = _dilated_constants()
    shaped = lambda a: a.reshape(batch, SEQ, A_WIDTH)
    out = pl.pallas_call(
        _dilated_kernel, grid=(batch,),
        in_specs=[seq, seq, seq, _resident(pf.shape), _resident(pb.shape), _resident(ex.shape)],
        out_specs=seq,
        out_shape=jax.ShapeDtypeStruct((batch, SEQ, A_WIDTH), BF16),
        scratch_shapes=[pltpu.VMEM((3, SEQ, A_WIDTH), BF16), pltpu.VMEM((3, SEQ, A_WIDTH), BF16),
                        pltpu.VMEM((SEQ, LANES), F32), pltpu.VMEM((2, SEQ, 2 * LANES), BF16)],
        compiler_params=_params(), name="dilated",
    )(shaped(qa), shaped(ka), shaped(va), pf, pb, ex)
    return out.reshape(batch * SEQ, A_WIDTH)


GQ_TQ = 1024


def _gqa_kernel(q_ref, k_ref, v_ref, o_ref, ksw_ref, vm_ref):
    lane = lax.broadcasted_iota(jnp.int32, (SEQ, LANES), 1)
    lo = lane < HEAD_DIM
    k = k_ref[...]
    v = v_ref[...]
    ksw_ref[...] = pltpu.roll(k.astype(F32), HEAD_DIM, 1).astype(BF16)
    vsw = pltpu.roll(v.astype(F32), HEAD_DIM, 1).astype(BF16)
    zero = jnp.zeros_like(v)
    vm_ref[0, :, :LANES] = jnp.where(lo, v, zero)
    vm_ref[1, :, :LANES] = jnp.where(lo, zero, vsw)
    vm_ref[2, :, :LANES] = jnp.where(lo, vsw, zero)
    vm_ref[3, :, :LANES] = jnp.where(lo, zero, v)
    for idx in range(4):
        vm_ref[idx, :, LANES:] = jnp.ones((SEQ, LANES), BF16)
    lo_q = lax.broadcasted_iota(jnp.int32, (1, LANES), 1) < HEAD_DIM

    nh = 2 * (C_WIDTH // LANES)

    def blk(i, carry):
        rows = pl.ds(pl.multiple_of(i * GQ_TQ, GQ_TQ), GQ_TQ)

        def scores(h):
            j, half = divmod(h, 2)
            qj = q_ref[rows, j * LANES:(j + 1) * LANES]
            sel = lo_q if half == 0 else jnp.logical_not(lo_q)
            qm = jnp.where(sel, qj, jnp.zeros_like(qj))
            kk = k_ref[...] if h // 3 == half else ksw_ref[...]
            return lax.dot_general(qm, kk, (((1,), (1,)), ((), ())), preferred_element_type=F32)

        def softmax(s):
            return jnp.exp2(s - jnp.max(s, axis=-1, keepdims=True)).astype(BF16)

        def values(h, pb):
            d = jnp.dot(pb, vm_ref[2 * (h // 3) + h % 2], preferred_element_type=F32)
            return d[:, :LANES] * (1.0 / d[:, LANES:])

        pending = {0: scores(0), 1: scores(1)}
        outs = []
        for h in range(nh):
            pb = softmax(pending.pop(h))
            if h + 2 < nh:
                pending[h + 2] = scores(h + 2)
            outs.append(values(h, pb))
        for j in range(nh // 2):
            o_ref[rows, j * LANES:(j + 1) * LANES] = (outs[2 * j] + outs[2 * j + 1]).astype(BF16)
        return carry

    lax.fori_loop(0, SEQ // GQ_TQ, blk, 0)


def _gqa(qc, kc, vc, batch):
    spec = lambda width: pl.BlockSpec((None, SEQ, width), lambda b: (b, 0, 0))
    out = pl.pallas_call(
        _gqa_kernel, grid=(batch,),
        in_specs=[spec(C_WIDTH), spec(C_KV_WIDTH), spec(C_KV_WIDTH)],
        out_specs=spec(C_WIDTH),
        out_shape=jax.ShapeDtypeStruct((batch, SEQ, C_WIDTH), BF16),
        scratch_shapes=[pltpu.VMEM((SEQ, LANES), BF16), pltpu.VMEM((4, SEQ, 2 * LANES), BF16)],
        compiler_params=_params(), name="gqa",
    )(qc.reshape(batch, SEQ, C_WIDTH), kc.reshape(batch, SEQ, C_KV_WIDTH), vc.reshape(batch, SEQ, C_KV_WIDTH))
    return out.reshape(batch * SEQ, C_WIDTH)


def _filter_kernel(z_ref, trow_ref, w1t_ref, b1_ref, f0_ref, w2t_ref, b2_ref, f1_ref, w3t_ref, dec_ref, out_ref):
    L = SEQ
    dot = functools.partial(jnp.dot, precision=HIGHEST, preferred_element_type=F32)
    h = jnp.sin(f0_ref[...] * (dot(w1t_ref[...], z_ref[...]) + b1_ref[...]))
    h = jnp.sin(f1_ref[...] * (dot(w2t_ref[...], h) + b2_ref[...]))
    trow = trow_ref[...]
    col = lax.broadcasted_iota(jnp.int32, (HY_WIDTH, L), 1)
    for order in range(2):
        fwd = slice(2 * order * HY_WIDTH, (2 * order + 1) * HY_WIDTH)
        bwd = slice((2 * order + 1) * HY_WIDTH, (2 * order + 2) * HY_WIDTH)
        hf = dot(w3t_ref[fwd, :], h[:, L:]) * jnp.exp(-trow[:, L:] * dec_ref[fwd, :])
        hb = dot(w3t_ref[bwd, :], h[:, :L]) * jnp.exp(-trow[:, :L] * dec_ref[bwd, :])
        hb = jnp.where(col == 0, 0.0, pltpu.roll(hb, 1, 1))
        norm = jnp.sum(jnp.abs(hf), axis=-1, keepdims=True) + jnp.sum(jnp.abs(hb), axis=-1, keepdims=True)
        rows = slice(order * HY_WIDTH, (order + 1) * HY_WIDTH)
        out_ref[rows, :L] = hb / norm
        out_ref[rows, L:] = hf / norm


def _filters(zt, trow, w1, b1, freq, w2, b2, w3, decay):
    colv = lambda a: a.reshape(-1, 1).astype(F32)
    w1t = jnp.zeros((HY_HIDDEN, LANES), F32).at[:, :HY_EMB].set(w1.T)
    args = (zt, trow, w1t, colv(b1), colv(freq[0]), w2.T, colv(b2), colv(freq[1]), w3.T, colv(decay))
    return pl.pallas_call(
        _filter_kernel, grid=(1,),
        in_specs=[_resident(a.shape) for a in args],
        out_specs=pl.BlockSpec((2 * HY_WIDTH, 2 * SEQ), lambda _: (0, 0)),
        out_shape=jax.ShapeDtypeStruct((2 * HY_WIDTH, 2 * SEQ), F32),
        compiler_params=_params(), name="filters",
    )(*args)


HY_TB = 256
HY_NB = SEQ // HY_TB


HY_CH = 8


def _hyena_kernel(cw_ref, cb_ref, d_ref, pv_ref, px1_ref, px2_ref, g0_ref, g1_ref, o_ref, gsh_ref, ust_ref, acc_ref):
    c0 = pl.program_id(0) * HY_CH
    nb = pv_ref.shape[1]
    col = lax.broadcasted_iota(jnp.int32, (nb, SEQ), 1)
    chans = range(HY_CH)

    def build_strip(cc, order):
        g = jnp.broadcast_to((g0_ref, g1_ref)[order][cc:cc + 1, :], (BF16_ROWS, 2 * SEQ))
        base = pltpu.bitcast(pltpu.roll(g, 0, 1, stride=1, stride_axis=0).astype(BF16), jnp.uint32)
        for a in range(LANES // BF16_ROWS):
            rows = base if a == 0 else pltpu.roll(base, BF16_ROWS * a, 1)
            gsh_ref[2 * cc + order, BF16_ROWS * a:BF16_ROWS * (a + 1), :] = pltpu.bitcast(rows, BF16)

    def dwconv(p_ref, cc, ch):
        p = p_ref[cc].astype(F32)
        prev = jnp.where(col == 0, 0.0, pltpu.roll(p, 1, 1))
        nxt = jnp.where(col == SEQ - 1, 0.0, pltpu.roll(p, SEQ - 1, 1))
        return prev * cw_ref[0, ch] + p * cw_ref[1, ch] + nxt * cw_ref[2, ch] + cb_ref[ch]

    def longconv(cc, order, u):
        gsh = gsh_ref.at[2 * cc + order]
        ust, acc = ust_ref.at[cc], acc_ref.at[cc]
        ub = u.astype(BF16)
        for tb in range(HY_NB):
            ust[tb * nb:(tb + 1) * nb, :] = ub[:, tb * HY_TB:(tb + 1) * HY_TB]
        acc[...] = jnp.zeros(acc.shape, F32)
        for dl in range(-(HY_NB - 1), HY_NB):
            x0 = HY_TB * (dl + HY_NB)
            rhs = jnp.concatenate([gsh[:, x0:x0 + HY_TB], gsh[:, x0 - LANES:x0 - LANES + HY_TB]], axis=0)
            b0, b1 = max(0, -dl), min(HY_NB, HY_NB - dl)
            acc[(b0 + dl) * nb:(b1 + dl) * nb, :] += jnp.dot(ust[b0 * nb:b1 * nb, :], rhs, preferred_element_type=F32)
        y = jnp.concatenate([acc[tb * nb:(tb + 1) * nb, :] for tb in range(HY_NB)], axis=1)
        return y + d_ref[order, c0 + cc] * u

    y1, y2 = [], []
    for cc in chans:
        build_strip(cc, 0)
        y1.append(longconv(cc, 0, dwconv(pv_ref, cc, c0 + cc)))
    for cc in chans:
        build_strip(cc, 1)
        y2.append(longconv(cc, 1, dwconv(px1_ref, cc, HY_WIDTH + c0 + cc) * y1[cc]))
    for cc in chans:
        o_ref[cc] = (dwconv(px2_ref, cc, 2 * HY_WIDTH + c0 + cc) * y2[cc]).astype(BF16)


def _hyena(pt, gt, conv_w, conv_b, dbias):
    nb = pt.shape[1]
    smem = pl.BlockSpec(memory_space=pltpu.SMEM)
    per_group = HY_WIDTH // HY_CH
    chan = lambda group: pl.BlockSpec((HY_CH, nb, SEQ), lambda c: (c + group * per_group, 0, 0))
    filt = lambda group: pl.BlockSpec((HY_CH, 2 * SEQ), lambda c: (c + group * per_group, 0))
    return pl.pallas_call(
        _hyena_kernel, grid=(per_group,),
        in_specs=[smem, smem, smem, chan(0), chan(1), chan(2), filt(0), filt(1)],
        out_specs=pl.BlockSpec((HY_CH, nb, SEQ), lambda c: (c, 0, 0)),
        out_shape=jax.ShapeDtypeStruct((HY_WIDTH, nb, SEQ), BF16),
        scratch_shapes=[pltpu.VMEM((2 * HY_CH, LANES, 2 * SEQ), BF16), pltpu.VMEM((HY_CH, HY_NB * nb, HY_TB), BF16),
                        pltpu.VMEM((HY_CH, HY_NB * nb, HY_TB), F32)],
        compiler_params=_params(), name="hyena",
    )(conv_w, conv_b, dbias, pt, pt, pt, gt, gt)


OP_TM = 2048
OP_RB = 512


def _outproj_kernel(oa_ref, hy_ref, oc_ref, w_ref, x_ref, g_ref, out_ref):
    def sub(i, carry):
        blocks = [pl.ds(pl.multiple_of((2 * i + u) * OP_RB, OP_RB), OP_RB) for u in range(2)]
        mixes = [jnp.dot(jnp.concatenate([oa_ref[rows, :], hy_ref[rows, :], oc_ref[rows, :]], axis=1), w_ref[...],
                         preferred_element_type=F32) for rows in blocks]
        for rows, mix in zip(blocks, mixes):
            out_ref[rows, :] = x_ref[rows, :] + _rms(mix, g_ref[...])
        return carry

    lax.fori_loop(0, OP_TM // (2 * OP_RB), sub, 0)


def _outproj(oa, hyo, oc, w, layer, x2, g):
    n = x2.shape[0]
    row = lambda width: pl.BlockSpec((OP_TM, width), lambda i: (i, 0))
    return pl.pallas_call(
        _outproj_kernel, grid=(n // OP_TM,),
        in_specs=[row(A_WIDTH), row(HY_WIDTH), row(C_WIDTH),
                  _layer_resident((D_MODEL, D_MODEL), layer), row(D_MODEL), _resident((1, D_MODEL))],
        out_specs=row(D_MODEL),
        out_shape=jax.ShapeDtypeStruct((n, D_MODEL), F32),
        compiler_params=_params(), name="outproj",
    )(oa, hyo, oc, w, x2, g)


FF_TM = 1024
FF_RB = 256
FF_HALO = 8


def _ffn_kernel(x_ref, xp_ref, xn_ref, gpre_ref, wg_ref, wu_ref, cw_ref, cb_ref, wd_ref, gpost_ref, out_ref, xs_ref):
    i = pl.program_id(0)
    tiles_per_seq = SEQ // FF_TM
    first = (i % tiles_per_seq) == 0
    last = (i % tiles_per_seq) == tiles_per_seq - 1
    xs_ref[:FF_HALO, :] = jnp.where(first, 0.0, xp_ref[...])
    xs_ref[FF_HALO:FF_HALO + FF_TM, :] = x_ref[...]
    xs_ref[FF_HALO + FF_TM:, :] = jnp.where(last, 0.0, xn_ref[...])

    acts, xs = [], []
    for s in range(FF_TM // FF_RB):
        xh = xs_ref[s * FF_RB:s * FF_RB + FF_RB + 2 * FF_HALO, :]
        x = xh[FF_HALO:FF_HALO + FF_RB]
        gate = jnp.dot(_rms(xh, gpre_ref[...]).astype(BF16), wg_ref[...], preferred_element_type=F32)
        up = jnp.dot(_rms(x, gpre_ref[...]).astype(BF16), wu_ref[...], preferred_element_type=F32)
        gc = (gate[FF_HALO - 1:FF_HALO - 1 + FF_RB] * cw_ref[0:1, :]
              + gate[FF_HALO:FF_HALO + FF_RB] * cw_ref[1:2, :]
              + gate[FF_HALO + 1:FF_HALO + 1 + FF_RB] * cw_ref[2:3, :] + cb_ref[...])
        acts.append((jax.nn.gelu(gc, approximate=True) * up).astype(BF16))
        xs.append(x)
    for s, (act, x) in enumerate(zip(acts, xs)):
        f = jnp.dot(act, wd_ref[...], preferred_element_type=F32)
        out_ref[s * FF_RB:(s + 1) * FF_RB, :] = x + _rms(f, gpost_ref[...])


def _ffn(x2, gpre, wg, wu, cw, cb, wd, gpost, layer):
    n = x2.shape[0]
    per = FF_TM // FF_HALO
    last_blk = n // FF_HALO - 1
    return pl.pallas_call(
        _ffn_kernel, grid=(n // FF_TM,),
        in_specs=[pl.BlockSpec((FF_TM, D_MODEL), lambda i: (i, 0)),
                  pl.BlockSpec((FF_HALO, D_MODEL), lambda i: (jnp.maximum(i * per - 1, 0), 0)),
                  pl.BlockSpec((FF_HALO, D_MODEL), lambda i: (jnp.minimum((i + 1) * per, last_blk), 0)),
                  _resident((1, D_MODEL)), _layer_resident((D_MODEL, D_FF), layer),
                  _layer_resident((D_MODEL, D_FF), layer), _resident((3, D_FF)), _resident((1, D_FF)),
                  _layer_resident((D_FF, D_MODEL), layer), _resident((1, D_MODEL))],
        out_specs=pl.BlockSpec((FF_TM, D_MODEL), lambda i: (i, 0)),
        out_shape=jax.ShapeDtypeStruct((n, D_MODEL), F32),
        scratch_shapes=[pltpu.VMEM((FF_TM + 2 * FF_HALO, D_MODEL), F32)],
        compiler_params=_params(), name="ffn",
    )(x2, x2, x2, gpre, wg, wu, cw, cb, wd, gpost)


def kernel(x, g_mix_pre, g_mix_post, g_ffn_pre, g_ffn_post, w_in, w_out, g_q, g_k, hy_conv_w, hy_conv_b, hy_w1, hy_b1, hy_freq, hy_w2, hy_b2, hy_w3, hy_decay, hy_d, ffn_w_gate, ffn_w_up, ffn_conv_w, ffn_conv_b, ffn_w_down):
    batch = x.shape[0]
    n = batch * SEQ
    tables = _rope_tables()
    bd = _head_mean_matrix()
    zt, trow = _hyena_positions()
    rowv = lambda a: a.reshape(1, -1).astype(F32)
    x2 = x.reshape(n, D_MODEL)
    w_in, w_out, ffn_w_gate, ffn_w_up, ffn_w_down = (
        w.astype(BF16) for w in (_inproj_column_order(w_in), w_out, ffn_w_gate, ffn_w_up, ffn_w_down))
    for i in range(DEPTH):
        qa, ka, va, hy, qc, kc, vc = _inproj(
            x2, rowv(g_mix_pre[i]), w_in, i, tables,
            rowv(jnp.tile(g_q[i], 2)), rowv(jnp.tile(g_k[i], 2)), bd)
        oa = _dilated(qa, ka, va, batch)
        oc = _gqa(qc, kc, vc, batch)
        gt = _filters(zt, trow, hy_w1[i], hy_b1[i], hy_freq[i], hy_w2[i], hy_b2[i], hy_w3[i], hy_decay[i])
        pt = jnp.transpose(hy.reshape(batch, SEQ, 3 * HY_WIDTH), (2, 0, 1))
        ot = _hyena(pt, gt, hy_conv_w[i].astype(F32), hy_conv_b[i].astype(F32), hy_d[i].astype(F32))
        hyo = jnp.transpose(ot, (1, 2, 0)).reshape(n, HY_WIDTH)
        x2 = _outproj(oa, hyo, oc, w_out, i, x2, rowv(g_mix_post[i]))
        x2 = _ffn(x2, rowv(g_ffn_pre[i]), ffn_w_gate, ffn_w_up, ffn_conv_w[i].astype(F32), rowv(ffn_conv_b[i]),
                  ffn_w_down, rowv(g_ffn_post[i]), i)
    return x2.reshape(batch, SEQ, D_MODEL)
```

```python
import functools
import math

import numpy as np
import jax
import jax.numpy as jnp
from jax import lax
from jax.experimental import pallas as pl
from jax.experimental.pallas import tpu as pltpu

F32 = jnp.float32
BF16 = jnp.bfloat16

D_MODEL = 1024
SEQ = 2048
DEPTH = 2
HEAD_DIM = 64
A_WIDTH = 384
HY_WIDTH = 256
C_WIDTH = 384
C_KV_WIDTH = 128
PROJ_WIDTH = 2560
GRID_W = 64
ROPE_THETA = 10000.0
HY_BANDS = 16
HY_EMB = 33
HY_HIDDEN = 64
D_FF = 2816
EPS = 1e-6
N_SIDE = 64
QSCALE = HEAD_DIM ** -0.5 * math.log2(math.e)
NEG = -1e30

LANES = 128
BF16_ROWS = 16
VMEM_LIMIT = 56 * 1024 * 1024

HIGHEST = lax.Precision.HIGHEST


def _params(n_grid_dims=1):
    return pltpu.CompilerParams(
        dimension_semantics=("arbitrary",) * n_grid_dims, vmem_limit_bytes=VMEM_LIMIT)


def _resident(shape):
    nd = len(shape)
    return pl.BlockSpec(shape, lambda *_: (0,) * nd, pipeline_mode=pl.Buffered(1))


def _layer_resident(shape, layer):
    nd = len(shape)
    return pl.BlockSpec((None,) + tuple(shape), lambda *_: (layer,) + (0,) * nd, pipeline_mode=pl.Buffered(1))


def _rope_tables():
    def angles(pos, dim):
        freqs = ROPE_THETA ** (-np.arange(0, dim, 2, dtype=np.float64) / dim)
        ang = pos.astype(np.float64)[:, None] * freqs[None, :]
        return np.cos(ang), np.sin(ang)

    pos = np.arange(SEQ)
    c, s = angles(pos, HEAD_DIM)
    cos_a = np.tile(np.concatenate([c, c], -1), (1, 2))
    sin_a = np.tile(np.concatenate([-s, s], -1), (1, 2))
    cr, sr = angles(pos // GRID_W, HEAD_DIM // 2)
    cc, sc = angles(pos % GRID_W, HEAD_DIM // 2)
    cos_c = np.tile(np.concatenate([cr, cr, cc, cc], -1), (1, 2))
    sin_c = np.tile(np.concatenate([-sr, sr, -sc, sc], -1), (1, 2))
    return tuple(jnp.asarray(t, F32) for t in (cos_a, sin_a, cos_c, sin_c))


def _head_mean_matrix():
    m = np.kron(np.eye(LANES // HEAD_DIM), np.full((HEAD_DIM, HEAD_DIM), 1.0 / HEAD_DIM))
    return jnp.asarray(np.concatenate([m, m], 0), BF16)


def _hyena_positions():
    L = SEQ
    t = np.linspace(0.0, 1.0, L)
    bands = np.linspace(1e-4, HY_BANDS - 1, HY_BANDS)
    ang = 2.0 * math.pi * bands[None, :] * np.arange(L)[:, None] / L
    z = np.concatenate([t[:, None], np.cos(ang), -np.sin(ang)], -1)
    zt = np.zeros((LANES, 2 * L))
    zt[:HY_EMB, :L] = z[::-1].T
    zt[:HY_EMB, L:] = z.T
    trow = np.concatenate([t[::-1], t])[None, :]
    return jnp.asarray(zt, F32), jnp.asarray(trow, F32)


def _rope(x, cos, sin_signed, half, lane):
    first = (lane & (2 * half - 1)) < half
    swapped = jnp.where(first, pltpu.roll(x, LANES - half, 1), pltpu.roll(x, half, 1))
    return x * cos + swapped * sin_signed


def _rms(x, gain):
    ms = jnp.mean(x * x, axis=-1, keepdims=True)
    return x * lax.rsqrt(ms + EPS) * gain


IN_TM = 2048
IN_RB = 512
IN_STEPS = IN_TM // IN_RB


def _inproj_kernel(x_ref, g_ref, w_ref, cosa_ref, sina_ref, cosc_ref, sinc_ref, gq_ref, gk_ref, bd_ref,
                   qa_ref, ka_ref, va_ref, hy_ref, qc_ref, kc_ref, vc_ref, h_ref):
    lane = lax.broadcasted_iota(jnp.int32, (IN_RB, LANES), 1)

    def normed(i):
        rows = pl.ds(pl.multiple_of(i * IN_RB, IN_RB), IN_RB)
        return _rms(x_ref[rows, :], g_ref[...]).astype(BF16)

    h_ref[0] = normed(0)

    def head_norm(blocks, gains):
        hi_lo = jnp.concatenate([jnp.concatenate(_split_hi_lo(v * v), axis=1) for v in blocks], axis=0)
        ms = jnp.dot(hi_lo, bd_ref[...], preferred_element_type=F32)
        return [v * lax.rsqrt(ms[i * IN_RB:(i + 1) * IN_RB] + EPS) * gain
                for i, (v, gain) in enumerate(zip(blocks, gains))]

    def sub(i, carry):
        rows = pl.ds(pl.multiple_of(i * IN_RB, IN_RB), IN_RB)
        slot = i % 2
        h = h_ref[slot]
        h_ref[1 - slot] = normed(jnp.minimum(i + 1, IN_STEPS - 1))
        cosa, sina = cosa_ref[rows, :], sina_ref[rows, :]
        cosc, sinc = cosc_ref[rows, :], sinc_ref[rows, :]
        half = PROJ_WIDTH // 2
        p1 = jnp.dot(h, w_ref[:, :half], preferred_element_type=F32)
        blk1 = lambda b: p1[:, b * LANES:(b + 1) * LANES]
        normed_qk = head_norm([blk1(b) for b in range(4)], [gq_ref[...]] * 3 + [gk_ref[...]])
        p2 = jnp.dot(h, w_ref[:, half:], preferred_element_type=F32)
        blk2 = lambda b: p2[:, b * LANES:(b + 1) * LANES]
        for b in range(A_WIDTH // LANES):
            cols = slice(b * LANES, (b + 1) * LANES)
            qc_ref[rows, cols] = (_rope(normed_qk[b], cosc, sinc, 16, lane) * QSCALE).astype(BF16)
            qa_ref[rows, cols] = (_rope(blk1(4 + b), cosa, sina, 32, lane) * QSCALE).astype(BF16)
            ka_ref[rows, cols] = _rope(blk1(7 + b), cosa, sina, 32, lane).astype(BF16)
            va_ref[rows, cols] = blk2(b).astype(BF16)
        kc_ref[rows, :] = _rope(normed_qk[3], cosc, sinc, 16, lane).astype(BF16)
        for b in range(3 * HY_WIDTH // LANES):
            hy_ref[rows, b * LANES:(b + 1) * LANES] = blk2(3 + b).astype(BF16)
        vc_ref[rows, :] = blk2(9).astype(BF16)
        return carry

    lax.fori_loop(0, IN_STEPS, sub, 0)


def _inproj_column_order(w):
    a_end, hy_end = 3 * A_WIDTH, 3 * A_WIDTH + 3 * HY_WIDTH
    c_qk_end = hy_end + C_WIDTH + C_KV_WIDTH
    return jnp.concatenate([w[..., hy_end:c_qk_end], w[..., :2 * A_WIDTH], w[..., 2 * A_WIDTH:hy_end],
                            w[..., c_qk_end:]], axis=-1)


def _inproj(x2, g, w, layer, tables, gq, gk, bd):
    n = x2.shape[0]
    tiles_per_seq = SEQ // IN_TM
    row = lambda width: pl.BlockSpec((IN_TM, width), lambda i: (i, 0))
    tab = pl.BlockSpec((IN_TM, LANES), lambda i: (i % tiles_per_seq, 0))
    outs = [A_WIDTH, A_WIDTH, A_WIDTH, 3 * HY_WIDTH, C_WIDTH, C_KV_WIDTH, C_KV_WIDTH]
    return pl.pallas_call(
        _inproj_kernel,
        grid=(n // IN_TM,),
        in_specs=[row(D_MODEL), _resident((1, D_MODEL)), _layer_resident((D_MODEL, PROJ_WIDTH), layer),
                  tab, tab, tab, tab, _resident((1, LANES)), _resident((1, LANES)), _resident((2 * LANES, LANES))],
        out_specs=[row(wd) for wd in outs],
        out_shape=[jax.ShapeDtypeStruct((n, wd), BF16) for wd in outs],
        scratch_shapes=[pltpu.VMEM((2, IN_RB, D_MODEL), BF16)],
        compiler_params=_params(),
        name="inproj",
    )(x2, g, w, *tables, gq, gk, bd)


DL_BQ = 128
DL_BK = 256
DL_PER_STEP = 2
DL_BLK = 256
DL_NBLK = SEQ // DL_BLK
DL_NPROB = SEQ // DL_BQ


DL_R = 16
DL_SEG = SEQ // DL_R


def _stat_lane(h):
    return HEAD_DIM * (h % 2) + h // 2


def _dilated_constants():
    per = DL_BLK // DL_R
    p = np.zeros((DL_BLK, DL_BLK))
    for i in range(per):
        for r in range(DL_R):
            p[r * per + i, DL_R * i + r] = 1.0
    e = np.zeros((LANES, A_WIDTH))
    for h in range(A_WIDTH // HEAD_DIM):
        e[_stat_lane(h), h * HEAD_DIM:(h + 1) * HEAD_DIM] = 1.0
    return jnp.asarray(p, BF16), jnp.asarray(p.T, BF16), jnp.asarray(np.concatenate([e, e], 0), BF16)


def _band_problems(probs, lane):
    lo = lax.broadcasted_iota(jnp.int32, (1, LANES), 1) < HEAD_DIM
    npair = A_WIDTH // LANES
    scores = []
    for q, k, _, _ in probs:
        nq = q.shape[0]
        ss = []
        for j in range(npair):
            cols = slice(j * LANES, (j + 1) * LANES)
            qj, zero = q[:, cols], jnp.zeros_like(q[:, cols])
            qst = jnp.concatenate([jnp.where(lo, qj, zero), jnp.where(lo, zero, qj)], axis=0)
            s2 = lax.dot_general(qst, k[:, cols], (((1,), (1,)), ((), ())), preferred_element_type=F32)
            ss += [s2[:nq], s2[nq:]]
        scores.append(ss)
    results = []
    for (q, _, v, mask), ss in zip(probs, scores):
        nq, nk = q.shape[0], v.shape[0]
        ones = jnp.ones((nk, LANES), BF16)
        ps = []
        mtile = jnp.zeros((nq, LANES), F32)
        for h, s in enumerate(ss):
            s = s + mask
            m = jnp.max(s, axis=-1, keepdims=True)
            ps.append(jnp.exp2(s - m).astype(BF16))
            mtile = jnp.where(lane == _stat_lane(h), m, mtile)
        outs = []
        ltile = jnp.ones((nq, LANES), F32)
        for j in range(npair):
            vext = jnp.concatenate([v[:, j * LANES:(j + 1) * LANES], ones], axis=1)
            d = jnp.dot(jnp.concatenate(ps[2 * j:2 * j + 2], axis=0), vext, preferred_element_type=F32)
            num0, den0, num1, den1 = d[:nq, :LANES], d[:nq, LANES:], d[nq:, :LANES], d[nq:, LANES:]
            outs.append(jnp.where(lo, num0 * (1.0 / den0), num1 * (1.0 / den1)))
            ltile = jnp.where(lane == j, den0, jnp.where(lane == HEAD_DIM + j, den1, ltile))
        results.append((jnp.concatenate(outs, axis=1), mtile + jnp.log2(ltile)))
    return results


def _split_hi_lo(x):
    hi = x.astype(BF16)
    return hi, (x - hi.astype(F32)).astype(BF16)


def _dilated_kernel(q_ref, k_ref, v_ref, pf_ref, pb_ref, ex_ref, out_ref, xp_ref, ob_ref, l1_ref, st_ref):
    lane = lax.broadcasted_iota(jnp.int32, (DL_BQ, LANES), 1)
    per = DL_BLK // DL_R

    def block_chunks(blk):
        return [pl.ds(pl.multiple_of(r * DL_SEG + blk * per, per), per) for r in range(DL_R)]

    def regroup(i, carry):
        blks = [2 * i, 2 * i + 1]
        ys = []
        for blk in blks:
            rows = pl.ds(pl.multiple_of(blk * DL_BLK, DL_BLK), DL_BLK)
            x = jnp.concatenate([q_ref[rows, :], k_ref[rows, :], v_ref[rows, :]], axis=1)
            ys.append(jnp.dot(pf_ref[...], x, preferred_element_type=F32).astype(BF16))
        for blk, y in zip(blks, ys):
            for r, at in enumerate(block_chunks(blk)):
                for a in range(3):
                    xp_ref[a, at, :] = y[r * per:(r + 1) * per, a * A_WIDTH:(a + 1) * A_WIDTH]
        return carry

    lax.fori_loop(0, DL_NBLK // 2, regroup, 0)

    def run(n_problems, per_step, gather, scatter):
        def step(i, carry):
            ids = [i * per_step + u for u in range(per_step)]
            for p, (o, lse) in zip(ids, _band_problems([gather(p) for p in ids], lane)):
                scatter(p, o.astype(BF16), lse)
            return carry

        lax.fori_loop(0, n_problems // per_step, step, 0)

    def band_mask(diff):
        return jnp.where((diff <= N_SIDE) & (diff >= -N_SIDE), 0.0, NEG)

    def put_regrouped(b, rows, o, lse):
        hi, lo = _split_hi_lo(lse)
        ob_ref[b, rows, :] = o
        st_ref[b - 1, rows, :LANES] = hi
        st_ref[b - 1, rows, LANES:] = lo

    row = lax.broadcasted_iota(jnp.int32, (DL_BQ, DL_BK), 0)
    col = lax.broadcasted_iota(jnp.int32, (DL_BQ, DL_BK), 1)

    def gather1(p):
        t0 = pl.multiple_of(p * DL_BQ, DL_BQ)
        ks = pl.multiple_of(jnp.clip(t0 - N_SIDE, 0, SEQ - DL_BK), N_SIDE)
        keys = pl.ds(ks, DL_BK)
        return q_ref[pl.ds(t0, DL_BQ), :], k_ref[keys, :], v_ref[keys, :], band_mask(col - row + (ks - t0))

    def scatter1(p, o, lse):
        rows = pl.ds(pl.multiple_of(p * DL_BQ, DL_BQ), DL_BQ)
        ob_ref[0, rows, :] = o
        l1_ref[rows, :] = lse

    run(DL_NPROB, DL_PER_STEP, gather1, scatter1)

    n4, qrows, krows = 4, DL_BQ // 4, DL_BK // 4
    slab_of = lambda idx, n: lax.shift_right_logical(idx, n.bit_length() - 1)
    ddiff = 4 * ((col & (krows - 1)) - (row & (qrows - 1))) + (slab_of(col, krows) - slab_of(row, qrows))

    def gather4(p):
        c, b = p // n4, p % n4
        q0 = pl.multiple_of(b * qrows, qrows)
        ks = pl.multiple_of(jnp.clip(q0 - N_SIDE // 4, 0, DL_SEG - krows), N_SIDE // 4)
        slabs = [(c + 4 * s) * DL_SEG for s in range(4)]
        take = lambda a, start, n: jnp.concatenate([xp_ref[a, pl.ds(s0 + start, n), :] for s0 in slabs], axis=0)
        return (take(0, q0, qrows), take(1, ks, krows), take(2, ks, krows), band_mask(ddiff + 4 * (ks - q0)))

    def scatter4(p, o, lse):
        c, b = p // n4, p % n4
        for s in range(4):
            rows = pl.ds(pl.multiple_of((c + 4 * s) * DL_SEG + b * qrows, qrows), qrows)
            put_regrouped(1, rows, o[s * qrows:(s + 1) * qrows], lse[s * qrows:(s + 1) * qrows])

    run(DL_NPROB, DL_PER_STEP, gather4, scatter4)

    mask16 = band_mask(lax.broadcasted_iota(jnp.int32, (DL_BQ, DL_SEG), 1)
                       - lax.broadcasted_iota(jnp.int32, (DL_BQ, DL_SEG), 0))

    def gather16(p):
        rows = pl.ds(pl.multiple_of(p * DL_SEG, DL_SEG), DL_SEG)
        return xp_ref[0, rows, :], xp_ref[1, rows, :], xp_ref[2, rows, :], mask16

    def scatter16(p, o, lse):
        put_regrouped(2, pl.ds(pl.multiple_of(p * DL_SEG, DL_SEG), DL_SEG), o, lse)

    run(DL_R, 2 * DL_PER_STEP, gather16, scatter16)

    def merge(i, carry):
        blks = [2 * i, 2 * i + 1]
        w0 = A_WIDTH + 2 * LANES
        nats = []
        for blk in blks:
            z = jnp.concatenate(
                [jnp.concatenate([ob_ref[1, at, :], st_ref[0, at, :], ob_ref[2, at, :], st_ref[1, at, :]], axis=1)
                 for at in block_chunks(blk)], axis=0)
            nats.append(jnp.dot(pb_ref[...], z, preferred_element_type=F32))
        weights = []
        for blk, nat in zip(blks, nats):
            rows = pl.ds(pl.multiple_of(blk * DL_BLK, DL_BLK), DL_BLK)
            s4 = nat[:, A_WIDTH:A_WIDTH + LANES] + nat[:, A_WIDTH + LANES:w0]
            s16 = nat[:, w0 + A_WIDTH:w0 + A_WIDTH + LANES] + nat[:, w0 + A_WIDTH + LANES:]
            s1 = l1_ref[rows, :]
            mx = jnp.maximum(jnp.maximum(s1, s4), s16)
            e1, e4, e16 = (jnp.exp2(s - mx) for s in (s1, s4, s16))
            inv = 1.0 / (e1 + e4 + e16)
            weights.append((e1 * inv, e4 * inv))
        spread = lambda w: jnp.dot(jnp.concatenate(_split_hi_lo(w), axis=1), ex_ref[...], preferred_element_type=F32)
        spreads = [(spread(a), spread(b)) for a, b in weights]
        for blk, nat, (w1, w4) in zip(blks, nats, spreads):
            rows = pl.ds(pl.multiple_of(blk * DL_BLK, DL_BLK), DL_BLK)
            o1, o4, o16 = ob_ref[0, rows, :].astype(F32), nat[:, :A_WIDTH], nat[:, w0:w0 + A_WIDTH]
            out_ref[rows, :] = (w1 * o1 + w4 * o4 + (1.0 - w1 - w4) * o16).astype(BF16)
        return carry

    lax.fori_loop(0, DL_NBLK // 2, merge, 0)


def _dilated(qa, ka, va, batch):
    seq = pl.BlockSpec((None, SEQ, A_WIDTH), lambda b: (b, 0, 0))
    pf, pb, ex = _dilated_constants()
    shaped = lambda a: a.reshape(batch, SEQ, A_WIDTH)
    out = pl.pallas_call(
        _dilated_kernel, grid=(batch,),
        in_specs=[seq, seq, seq, _resident(pf.shape), _resident(pb.shape), _resident(ex.shape)],
        out_specs=seq,
        out_shape=jax.ShapeDtypeStruct((batch, SEQ, A_WIDTH), BF16),
        scratch_shapes=[pltpu.VMEM((3, SEQ, A_WIDTH), BF16), pltpu.VMEM((3, SEQ, A_WIDTH), BF16),
                        pltpu.VMEM((SEQ, LANES), F32), pltpu.VMEM((2, SEQ, 2 * LANES), BF16)],
        compiler_params=_params(), name="dilated",
    )(shaped(qa), shaped(ka), shaped(va), pf, pb, ex)
    return out.reshape(batch * SEQ, A_WIDTH)


GQ_TQ = 1024


def _gqa_kernel(q_ref, k_ref, v_ref, o_ref, ksw_ref, vm_ref):
    lane = lax.broadcasted_iota(jnp.int32, (SEQ, LANES), 1)
    lo = lane < HEAD_DIM
    k = k_ref[...]
    v = v_ref[...]
    ksw_ref[...] = pltpu.roll(k.astype(F32), HEAD_DIM, 1).astype(BF16)
    vsw = pltpu.roll(v.astype(F32), HEAD_DIM, 1).astype(BF16)
    zero = jnp.zeros_like(v)
    vm_ref[0, :, :LANES] = jnp.where(lo, v, zero)
    vm_ref[1, :, :LANES] = jnp.where(lo, zero, vsw)
    vm_ref[2, :, :LANES] = jnp.where(lo, vsw, zero)
    vm_ref[3, :, :LANES] = jnp.where(lo, zero, v)
    for idx in range(4):
        vm_ref[idx, :, LANES:] = jnp.ones((SEQ, LANES), BF16)
    lo_q = lax.broadcasted_iota(jnp.int32, (1, LANES), 1) < HEAD_DIM

    nh = 2 * (C_WIDTH // LANES)

    def blk(i, carry):
        rows = pl.ds(i * GQ_TQ, GQ_TQ)

        def scores(h):
            j, half = divmod(h, 2)
            qj = q_ref[rows, j * LANES:(j + 1) * LANES]
            sel = lo_q if half == 0 else jnp.logical_not(lo_q)
            qm = jnp.where(sel, qj, jnp.zeros_like(qj))
            kk = k_ref[...] if h // 3 == half else ksw_ref[...]
            return lax.dot_general(qm, kk, (((1,), (1,)), ((), ())), preferred_element_type=F32)

        def softmax(s):
            return jnp.exp2(s - jnp.max(s, axis=-1, keepdims=True)).astype(BF16)

        def values(h, pb):
            d = jnp.dot(pb, vm_ref[2 * (h // 3) + h % 2], preferred_element_type=F32)
            return d[:, :LANES] * (1.0 / d[:, LANES:])

        pending = {0: scores(0), 1: scores(1)}
        outs = []
        for h in range(nh):
            pb = softmax(pending.pop(h))
            if h + 2 < nh:
                pending[h + 2] = scores(h + 2)
            outs.append(values(h, pb))
        for j in range(nh // 2):
            o_ref[rows, j * LANES:(j + 1) * LANES] = (outs[2 * j] + outs[2 * j + 1]).astype(BF16)
        return carry

    for i in range(SEQ // GQ_TQ):
        blk(i, 0)


def _gqa(qc, kc, vc, batch):
    spec = lambda width: pl.BlockSpec((None, SEQ, width), lambda b: (b, 0, 0))
    out = pl.pallas_call(
        _gqa_kernel, grid=(batch,),
        in_specs=[spec(C_WIDTH), spec(C_KV_WIDTH), spec(C_KV_WIDTH)],
        out_specs=spec(C_WIDTH),
        out_shape=jax.ShapeDtypeStruct((batch, SEQ, C_WIDTH), BF16),
        scratch_shapes=[pltpu.VMEM((SEQ, LANES), BF16), pltpu.VMEM((4, SEQ, 2 * LANES), BF16)],
        compiler_params=_params(), name="gqa",
    )(qc.reshape(batch, SEQ, C_WIDTH), kc.reshape(batch, SEQ, C_KV_WIDTH), vc.reshape(batch, SEQ, C_KV_WIDTH))
    return out.reshape(batch * SEQ, C_WIDTH)


def _filter_kernel(z_ref, trow_ref, w1t_ref, b1_ref, f0_ref, w2t_ref, b2_ref, f1_ref, w3t_ref, dec_ref, out_ref):
    L = SEQ
    dot = functools.partial(jnp.dot, precision=HIGHEST, preferred_element_type=F32)
    h = jnp.sin(f0_ref[...] * (dot(w1t_ref[...], z_ref[...]) + b1_ref[...]))
    h = jnp.sin(f1_ref[...] * (dot(w2t_ref[...], h) + b2_ref[...]))
    trow = trow_ref[...]
    col = lax.broadcasted_iota(jnp.int32, (HY_WIDTH, L), 1)
    for order in range(2):
        fwd = slice(2 * order * HY_WIDTH, (2 * order + 1) * HY_WIDTH)
        bwd = slice((2 * order + 1) * HY_WIDTH, (2 * order + 2) * HY_WIDTH)
        hf = dot(w3t_ref[fwd, :], h[:, L:]) * jnp.exp(-trow[:, L:] * dec_ref[fwd, :])
        hb = dot(w3t_ref[bwd, :], h[:, :L]) * jnp.exp(-trow[:, :L] * dec_ref[bwd, :])
        hb = jnp.where(col == 0, 0.0, pltpu.roll(hb, 1, 1))
        norm = jnp.sum(jnp.abs(hf), axis=-1, keepdims=True) + jnp.sum(jnp.abs(hb), axis=-1, keepdims=True)
        rows = slice(order * HY_WIDTH, (order + 1) * HY_WIDTH)
        out_ref[rows, :L] = hb / norm
        out_ref[rows, L:] = hf / norm


def _filters(zt, trow, w1, b1, freq, w2, b2, w3, decay):
    colv = lambda a: a.reshape(-1, 1).astype(F32)
    w1t = jnp.zeros((HY_HIDDEN, LANES), F32).at[:, :HY_EMB].set(w1.T)
    args = (zt, trow, w1t, colv(b1), colv(freq[0]), w2.T, colv(b2), colv(freq[1]), w3.T, colv(decay))
    return pl.pallas_call(
        _filter_kernel, grid=(1,),
        in_specs=[_resident(a.shape) for a in args],
        out_specs=pl.BlockSpec((2 * HY_WIDTH, 2 * SEQ), lambda _: (0, 0)),
        out_shape=jax.ShapeDtypeStruct((2 * HY_WIDTH, 2 * SEQ), F32),
        compiler_params=_params(), name="filters",
    )(*args)


HY_TB = 256
HY_NB = SEQ // HY_TB


HY_CH = 8


def _hyena_kernel(cw_ref, cb_ref, d_ref, pv_ref, px1_ref, px2_ref, g0_ref, g1_ref, o_ref, gsh_ref, ust_ref, acc_ref):
    c0 = pl.program_id(0) * HY_CH
    nb = pv_ref.shape[1]
    col = lax.broadcasted_iota(jnp.int32, (nb, SEQ), 1)
    chans = range(HY_CH)

    def build_strip(cc, order):
        g = jnp.broadcast_to((g0_ref, g1_ref)[order][cc:cc + 1, :], (BF16_ROWS, 2 * SEQ))
        base = pltpu.bitcast(pltpu.roll(g, 0, 1, stride=1, stride_axis=0).astype(BF16), jnp.uint32)
        for a in range(LANES // BF16_ROWS):
            rows = base if a == 0 else pltpu.roll(base, BF16_ROWS * a, 1)
            gsh_ref[2 * cc + order, BF16_ROWS * a:BF16_ROWS * (a + 1), :] = pltpu.bitcast(rows, BF16)

    def dwconv(p_ref, cc, ch):
        p = p_ref[cc].astype(F32)
        prev = jnp.where(col == 0, 0.0, pltpu.roll(p, 1, 1))
        nxt = jnp.where(col == SEQ - 1, 0.0, pltpu.roll(p, SEQ - 1, 1))
        return prev * cw_ref[0, ch] + p * cw_ref[1, ch] + nxt * cw_ref[2, ch] + cb_ref[ch]

    def longconv(cc, order, u):
        gsh = gsh_ref.at[2 * cc + order]
        ust, acc = ust_ref.at[cc], acc_ref.at[cc]
        ub = u.astype(BF16)
        for tb in range(HY_NB):
            ust[tb * nb:(tb + 1) * nb, :] = ub[:, tb * HY_TB:(tb + 1) * HY_TB]
        acc[...] = jnp.zeros(acc.shape, F32)
        for dl in range(-(HY_NB - 1), HY_NB):
            x0 = HY_TB * (dl + HY_NB)
            rhs = jnp.concatenate([gsh[:, x0:x0 + HY_TB], gsh[:, x0 - LANES:x0 - LANES + HY_TB]], axis=0)
            b0, b1 = max(0, -dl), min(HY_NB, HY_NB - dl)
            acc[(b0 + dl) * nb:(b1 + dl) * nb, :] += jnp.dot(ust[b0 * nb:b1 * nb, :], rhs, preferred_element_type=F32)
        y = jnp.concatenate([acc[tb * nb:(tb + 1) * nb, :] for tb in range(HY_NB)], axis=1)
        return y + d_ref[order, c0 + cc] * u

    y1, y2 = [], []
    for cc in chans:
        build_strip(cc, 0)
        y1.append(longconv(cc, 0, dwconv(pv_ref, cc, c0 + cc)))
    for cc in chans:
        build_strip(cc, 1)
        y2.append(longconv(cc, 1, dwconv(px1_ref, cc, HY_WIDTH + c0 + cc) * y1[cc]))
    for cc in chans:
        o_ref[cc] = (dwconv(px2_ref, cc, 2 * HY_WIDTH + c0 + cc) * y2[cc]).astype(BF16)


def _hyena(pt, gt, conv_w, conv_b, dbias):
    nb = pt.shape[1]
    smem = pl.BlockSpec(memory_space=pltpu.SMEM)
    per_group = HY_WIDTH // HY_CH
    chan = lambda group: pl.BlockSpec((HY_CH, nb, SEQ), lambda c: (c + group * per_group, 0, 0))
    filt = lambda group: pl.BlockSpec((HY_CH, 2 * SEQ), lambda c: (c + group * per_group, 0))
    return pl.pallas_call(
        _hyena_kernel, grid=(per_group,),
        in_specs=[smem, smem, smem, chan(0), chan(1), chan(2), filt(0), filt(1)],
        out_specs=pl.BlockSpec((HY_CH, nb, SEQ), lambda c: (c, 0, 0)),
        out_shape=jax.ShapeDtypeStruct((HY_WIDTH, nb, SEQ), BF16),
        scratch_shapes=[pltpu.VMEM((2 * HY_CH, LANES, 2 * SEQ), BF16), pltpu.VMEM((HY_CH, HY_NB * nb, HY_TB), BF16),
                        pltpu.VMEM((HY_CH, HY_NB * nb, HY_TB), F32)],
        compiler_params=_params(), name="hyena",
    )(conv_w, conv_b, dbias, pt, pt, pt, gt, gt)


OP_TM = 2048
OP_RB = 512


def _outproj_kernel(oa_ref, hy_ref, oc_ref, w_ref, x_ref, g_ref, out_ref):
    def sub(i, carry):
        blocks = [pl.ds(pl.multiple_of((2 * i + u) * OP_RB, OP_RB), OP_RB) for u in range(2)]
        mixes = [jnp.dot(jnp.concatenate([oa_ref[rows, :], hy_ref[rows, :], oc_ref[rows, :]], axis=1), w_ref[...],
                         preferred_element_type=F32) for rows in blocks]
        for rows, mix in zip(blocks, mixes):
            out_ref[rows, :] = x_ref[rows, :] + _rms(mix, g_ref[...])
        return carry

    lax.fori_loop(0, OP_TM // (2 * OP_RB), sub, 0)


def _outproj(oa, hyo, oc, w, layer, x2, g):
    n = x2.shape[0]
    row = lambda width: pl.BlockSpec((OP_TM, width), lambda i: (i, 0))
    return pl.pallas_call(
        _outproj_kernel, grid=(n // OP_TM,),
        in_specs=[row(A_WIDTH), row(HY_WIDTH), row(C_WIDTH),
                  _layer_resident((D_MODEL, D_MODEL), layer), row(D_MODEL), _resident((1, D_MODEL))],
        out_specs=row(D_MODEL),
        out_shape=jax.ShapeDtypeStruct((n, D_MODEL), F32),
        compiler_params=_params(), name="outproj",
    )(oa, hyo, oc, w, x2, g)


FF_TM = 1024
FF_RB = 256
FF_HALO = 8


def _ffn_kernel(x_ref, xp_ref, xn_ref, gpre_ref, wg_ref, wu_ref, cw_ref, cb_ref, wd_ref, gpost_ref, out_ref, xs_ref):
    i = pl.program_id(0)
    tiles_per_seq = SEQ // FF_TM
    first = (i % tiles_per_seq) == 0
    last = (i % tiles_per_seq) == tiles_per_seq - 1
    xs_ref[:FF_HALO, :] = jnp.where(first, 0.0, xp_ref[...])
    xs_ref[FF_HALO:FF_HALO + FF_TM, :] = x_ref[...]
    xs_ref[FF_HALO + FF_TM:, :] = jnp.where(last, 0.0, xn_ref[...])

    acts, xs = [], []
    for s in range(FF_TM // FF_RB):
        xh = xs_ref[s * FF_RB:s * FF_RB + FF_RB + 2 * FF_HALO, :]
        x = xh[FF_HALO:FF_HALO + FF_RB]
        gate = jnp.dot(_rms(xh, gpre_ref[...]).astype(BF16), wg_ref[...], preferred_element_type=F32)
        up = jnp.dot(_rms(x, gpre_ref[...]).astype(BF16), wu_ref[...], preferred_element_type=F32)
        gc = (gate[FF_HALO - 1:FF_HALO - 1 + FF_RB] * cw_ref[0:1, :]
              + gate[FF_HALO:FF_HALO + FF_RB] * cw_ref[1:2, :]
              + gate[FF_HALO + 1:FF_HALO + 1 + FF_RB] * cw_ref[2:3, :] + cb_ref[...])
        acts.append((jax.nn.gelu(gc, approximate=True) * up).astype(BF16))
        xs.append(x)
    for s, (act, x) in enumerate(zip(acts, xs)):
        f = jnp.dot(act, wd_ref[...], preferred_element_type=F32)
        out_ref[s * FF_RB:(s + 1) * FF_RB, :] = x + _rms(f, gpost_ref[...])


def _ffn(x2, gpre, wg, wu, cw, cb, wd, gpost, layer):
    n = x2.shape[0]
    per = FF_TM // FF_HALO
    last_blk = n // FF_HALO - 1
    return pl.pallas_call(
        _ffn_kernel, grid=(n // FF_TM,),
        in_specs=[pl.BlockSpec((FF_TM, D_MODEL), lambda i: (i, 0)),
                  pl.BlockSpec((FF_HALO, D_MODEL), lambda i: (jnp.maximum(i * per - 1, 0), 0)),
                  pl.BlockSpec((FF_HALO, D_MODEL), lambda i: (jnp.minimum((i + 1) * per, last_blk), 0)),
                  _resident((1, D_MODEL)), _layer_resident((D_MODEL, D_FF), layer),
                  _layer_resident((D_MODEL, D_FF), layer), _resident((3, D_FF)), _resident((1, D_FF)),
                  _layer_resident((D_FF, D_MODEL), layer), _resident((1, D_MODEL))],
        out_specs=pl.BlockSpec((FF_TM, D_MODEL), lambda i: (i, 0)),
        out_shape=jax.ShapeDtypeStruct((n, D_MODEL), F32),
        scratch_shapes=[pltpu.VMEM((FF_TM + 2 * FF_HALO, D_MODEL), F32)],
        compiler_params=_params(), name="ffn",
    )(x2, x2, x2, gpre, wg, wu, cw, cb, wd, gpost)


def kernel(x, g_mix_pre, g_mix_post, g_ffn_pre, g_ffn_post, w_in, w_out, g_q, g_k, hy_conv_w, hy_conv_b, hy_w1, hy_b1, hy_freq, hy_w2, hy_b2, hy_w3, hy_decay, hy_d, ffn_w_gate, ffn_w_up, ffn_conv_w, ffn_conv_b, ffn_w_down):
    batch = x.shape[0]
    n = batch * SEQ
    tables = _rope_tables()
    bd = _head_mean_matrix()
    zt, trow = _hyena_positions()
    rowv = lambda a: a.reshape(1, -1).astype(F32)
    x2 = x.reshape(n, D_MODEL)
    w_in, w_out, ffn_w_gate, ffn_w_up, ffn_w_down = (
        w.astype(BF16) for w in (_inproj_column_order(w_in), w_out, ffn_w_gate, ffn_w_up, ffn_w_down))
    for i in range(DEPTH):
        qa, ka, va, hy, qc, kc, vc = _inproj(
            x2, rowv(g_mix_pre[i]), w_in, i, tables,
            rowv(jnp.tile(g_q[i], 2)), rowv(jnp.tile(g_k[i], 2)), bd)
        oa = _dilated(qa, ka, va, batch)
        oc = _gqa(qc, kc, vc, batch)
        gt = _filters(zt, trow, hy_w1[i], hy_b1[i], hy_freq[i], hy_w2[i], hy_b2[i], hy_w3[i], hy_decay[i])
        pt = jnp.transpose(hy.reshape(batch, SEQ, 3 * HY_WIDTH), (2, 0, 1))
        ot = _hyena(pt, gt, hy_conv_w[i].astype(F32), hy_conv_b[i].astype(F32), hy_d[i].astype(F32))
        hyo = jnp.transpose(ot, (1, 2, 0)).reshape(n, HY_WIDTH)
        x2 = _outproj(oa, hyo, oc, w_out, i, x2, rowv(g_mix_post[i]))
        x2 = _ffn(x2, rowv(g_ffn_pre[i]), ffn_w_gate, ffn_w_up, ffn_conv_w[i].astype(F32), rowv(ffn_conv_b[i]),
                  ffn_w_down, rowv(g_ffn_post[i]), i)
    return x2.reshape(batch, SEQ, D_MODEL)
```
